```python
import jax, jax.numpy as jnp
from jax import lax
import numpy as np

D_MODEL = 2048
BATCH = 4
SEQ = 2048
DEPTH = 4

N_A = DEPTH // 2
N_B = DEPTH - N_A
A_HEADS = 16
Q_LORA = 512
KV_LORA = 512
NOPE_DIM = 128
ROPE_DIM = 64
V_DIM = 128
ROPE_THETA = 10000.0
B_HEADS = 16
B_HEAD_DIM = 128
D_FF = 4 * D_MODEL
Q_BLOCK = 128
NORM_EPS = 1e-6

kernel_name = "yoco_mla_stickbreaking_trunk"


def rms_norm(x, g):
    xf = x.astype(jnp.float32)
    y = xf * lax.rsqrt(jnp.mean(xf * xf, axis=-1, keepdims=True) + NORM_EPS)
    return (y * g.astype(jnp.float32)).astype(x.dtype)


def rope_tables(seq_len):
    pos = jnp.arange(seq_len, dtype=jnp.float32)
    inv_freq = ROPE_THETA ** (-jnp.arange(0, ROPE_DIM, 2, dtype=jnp.float32) / ROPE_DIM)
    ang = pos[:, None] * inv_freq[None, :]
    return jnp.cos(ang), jnp.sin(ang)


def apply_rope(x, cos, sin):
    c = cos[None, :, None, :].astype(x.dtype)
    s = sin[None, :, None, :].astype(x.dtype)
    x1, x2 = x[..., : ROPE_DIM // 2], x[..., ROPE_DIM // 2:]
    return jnp.concatenate([x1 * c - x2 * s, x1 * s + x2 * c], axis=-1)


def causal_softmax_attention(q, k, v, scale):
    bsz, seq, heads, dv = v.shape
    n_blocks = seq // Q_BLOCK
    kpos = jnp.arange(seq)

    def block(i):
        start = i * Q_BLOCK
        qb = lax.dynamic_slice_in_dim(q, start, Q_BLOCK, axis=1)
        s = jnp.einsum('bqhd,bkhd->bhqk', qb, k).astype(jnp.float32) * scale
        qpos = start + jnp.arange(Q_BLOCK)
        mask = kpos[None, :] <= qpos[:, None]
        p = jax.nn.softmax(jnp.where(mask, s, -jnp.inf), axis=-1)
        return jnp.einsum('bhqk,bkhd->bqhd', p.astype(v.dtype), v)

    out = lax.map(block, jnp.arange(n_blocks))
    return out.transpose(1, 0, 2, 3, 4).reshape(bsz, seq, heads, dv)


def stick_breaking_attention(q, k, v, scale):
    bsz, seq, heads, dv = v.shape
    n_blocks = seq // Q_BLOCK
    kpos = jnp.arange(seq)

    def block(i):
        start = i * Q_BLOCK
        qb = lax.dynamic_slice_in_dim(q, start, Q_BLOCK, axis=1)
        z = jnp.einsum('bqhd,bkhd->bhqk', qb, k).astype(jnp.float32) * scale
        qpos = start + jnp.arange(Q_BLOCK)
        strict = kpos[None, :] < qpos[:, None]
        log_beta = jax.nn.log_sigmoid(z)
        log_1m = jnp.where(strict, log_beta - z, 0.0)
        suffix = lax.cumsum(log_1m, axis=3, reverse=True) - log_1m
        a = jnp.where(strict, jnp.exp(log_beta + suffix), 0.0)
        return jnp.einsum('bhqk,bkhd->bqhd', a.astype(v.dtype), v)

    out = lax.map(block, jnp.arange(n_blocks))
    return out.transpose(1, 0, 2, 3, 4).reshape(bsz, seq, heads, dv)


def mla(h, w_dq_dkv, q_norm_g, kv_norm_g, w_uq, w_ukv, w_o, cos, sin):
    bsz, seq, _ = h.shape
    lat = h @ w_dq_dkv
    c_q = lat[..., :Q_LORA]
    c_kv = lat[..., Q_LORA:Q_LORA + KV_LORA]
    k_rope = lat[..., Q_LORA + KV_LORA:]
    q = (rms_norm(c_q, q_norm_g) @ w_uq).reshape(bsz, seq, A_HEADS, NOPE_DIM + ROPE_DIM)
    q = jnp.concatenate([q[..., :NOPE_DIM], apply_rope(q[..., NOPE_DIM:], cos, sin)], axis=-1)
    kv = (rms_norm(c_kv, kv_norm_g) @ w_ukv).reshape(bsz, seq, A_HEADS, NOPE_DIM + V_DIM)
    k_nope, v = kv[..., :NOPE_DIM], kv[..., NOPE_DIM:]
    k_rope = apply_rope(k_rope[:, :, None, :], cos, sin)
    k = jnp.concatenate([k_nope, jnp.broadcast_to(k_rope, (bsz, seq, A_HEADS, ROPE_DIM))], axis=-1)
    o = causal_softmax_attention(q, k, v, (NOPE_DIM + ROPE_DIM) ** -0.5)
    return o.reshape(bsz, seq, A_HEADS * V_DIM) @ w_o


def sq_relu_mlp(h, w_up, w_down):
    return jnp.square(jax.nn.relu(h @ w_up)) @ w_down


def setup_inputs(seed: int = 0) -> dict:
    key = jax.random.key(seed)
    ks = jax.random.split(key, 24)
    f32 = jnp.float32

    def w(k, shape, fan_in):
        return jax.random.normal(k, shape, f32) * (fan_in ** -0.5)

    def gain(k, shape):
        return 1.0 + 0.05 * jax.random.normal(k, shape, f32)

    return {
        "x": jax.random.normal(ks[0], (BATCH, SEQ, D_MODEL), f32),
        "a_pre_g": gain(ks[1], (N_A, D_MODEL)),
        "a_post_g": gain(ks[2], (N_A, D_MODEL)),
        "a_w_dq_dkv": w(ks[3], (N_A, D_MODEL, Q_LORA + KV_LORA + ROPE_DIM), D_MODEL),
        "a_q_norm_g": gain(ks[4], (N_A, Q_LORA)),
        "a_kv_norm_g": gain(ks[5], (N_A, KV_LORA)),
        "a_w_uq": w(ks[6], (N_A, Q_LORA, A_HEADS * (NOPE_DIM + ROPE_DIM)), Q_LORA),
        "a_w_ukv": w(ks[7], (N_A, KV_LORA, A_HEADS * (NOPE_DIM + V_DIM)), KV_LORA),
        "a_w_o": w(ks[8], (N_A, A_HEADS * V_DIM, D_MODEL), A_HEADS * V_DIM),
        "kv_norm_g": gain(ks[9], (D_MODEL,)),
        "w_kv_shared": w(ks[10], (D_MODEL, 2 * B_HEADS * B_HEAD_DIM), D_MODEL),
        "b_pre_g": gain(ks[11], (N_B, D_MODEL)),
        "b_post_g": gain(ks[12], (N_B, D_MODEL)),
        "b_w_q": w(ks[13], (N_B, D_MODEL, B_HEADS * B_HEAD_DIM), D_MODEL),
        "b_w_o": w(ks[14], (N_B, B_HEADS * B_HEAD_DIM, D_MODEL), B_HEADS * B_HEAD_DIM),
        "mlp_pre_g": gain(ks[15], (DEPTH, D_MODEL)),
        "mlp_post_g": gain(ks[16], (DEPTH, D_MODEL)),
        "mlp_w_up": w(ks[17], (DEPTH, D_MODEL, D_FF), D_MODEL),
        "mlp_w_down": w(ks[18], (DEPTH, D_FF, D_MODEL), D_FF),
    }


def reference(x, a_pre_g, a_post_g, a_w_dq_dkv, a_q_norm_g, a_kv_norm_g, a_w_uq, a_w_ukv, a_w_o,
              kv_norm_g, w_kv_shared, b_pre_g, b_post_g, b_w_q, b_w_o,
              mlp_pre_g, mlp_post_g, mlp_w_up, mlp_w_down):
    bsz, seq, _ = x.shape
    cos, sin = rope_tables(seq)
    sb_scale = B_HEAD_DIM ** -0.5
    h = x
    k_sh = v_sh = None
    for layer in range(DEPTH):
        if layer < N_A:
            i = layer
            a = mla(rms_norm(h, a_pre_g[i]), a_w_dq_dkv[i], a_q_norm_g[i], a_kv_norm_g[i],
                    a_w_uq[i], a_w_ukv[i], a_w_o[i], cos, sin)
            h = h + rms_norm(a, a_post_g[i])
        else:
            j = layer - N_A
            q = (rms_norm(h, b_pre_g[j]) @ b_w_q[j]).reshape(bsz, seq, B_HEADS, B_HEAD_DIM)
            o = stick_breaking_attention(q, k_sh, v_sh, sb_scale)
            a = o.reshape(bsz, seq, B_HEADS * B_HEAD_DIM) @ b_w_o[j]
            h = h + rms_norm(a, b_post_g[j])
        m = sq_relu_mlp(rms_norm(h, mlp_pre_g[layer]), mlp_w_up[layer], mlp_w_down[layer])
        h = h + rms_norm(m, mlp_post_g[layer])
        if layer == N_A - 1:
            kv = (rms_norm(h, kv_norm_g) @ w_kv_shared).reshape(bsz, seq, 2, B_HEADS, B_HEAD_DIM)
            k_sh, v_sh = kv[:, :, 0], kv[:, :, 1]
    return h
```

```python
import functools

import jax
import jax.numpy as jnp
from jax import lax
from jax.experimental import pallas as pl
from jax.experimental.pallas import tpu as pltpu

D_MODEL = 2048
A_HEADS = 16
Q_LORA = 512
KV_LORA = 512
NOPE_DIM = 128
ROPE_DIM = 64
V_DIM = 128
ROPE_THETA = 10000.0
B_HEADS = 16
B_HEAD_DIM = 128
NORM_EPS = 1e-6

LANES = 128
QK_PAD = 256
LAT_PAD = Q_LORA + KV_LORA + LANES
VMEM_LIMIT = 56 * 1024 * 1024

BF16 = jnp.bfloat16
F32 = jnp.float32


def _params(*sem):
    return pltpu.CompilerParams(dimension_semantics=sem, vmem_limit_bytes=VMEM_LIMIT)


def _rms(x, g):
    return x * lax.rsqrt(jnp.mean(x * x, axis=-1, keepdims=True) + NORM_EPS) * g


def _dot(a, b):
    return jnp.dot(a, b, preferred_element_type=F32)


def _dot_nt(a, b):
    return lax.dot_general(a, b, (((1,), (1,)), ((), ())), preferred_element_type=F32)


def _rope(x, c, sa, sb):
    return x * c + pltpu.roll(x, 96, 1) * sa + pltpu.roll(x, 32, 1) * sb


def _norm_matmul_kernel(x_ref, g_ref, w_ref, o_ref, xn_ref, *, scale):
    @pl.when(pl.program_id(1) == 0)
    def _():
        xn_ref[...] = _rms(x_ref[...], g_ref[...]).astype(BF16)

    y = _dot(xn_ref[...], w_ref[...])
    if scale != 1.0:
        y = y * scale
    o_ref[...] = y.astype(o_ref.dtype)


def norm_matmul(x, g, w, *, scale=1.0, tm=512, tn=1024):
    t, d = x.shape
    n = w.shape[1]
    return pl.pallas_call(
        functools.partial(_norm_matmul_kernel, scale=scale),
        grid=(t // tm, n // tn),
        in_specs=[
            pl.BlockSpec((tm, d), lambda i, j: (i, 0)),
            pl.BlockSpec((1, d), lambda i, j: (0, 0)),
            pl.BlockSpec((d, tn), lambda i, j: (0, j)),
        ],
        out_specs=pl.BlockSpec((tm, tn), lambda i, j: (i, j)),
        out_shape=jax.ShapeDtypeStruct((t, n), BF16),
        scratch_shapes=[pltpu.VMEM((tm, d), BF16)],
        compiler_params=_params("parallel", "arbitrary"),
        name="norm_matmul",
    )(x, g.reshape(1, d), w)


def _mla_front_kernel(h_ref, gpre_ref, wd_ref, gq_ref, gkv_ref, wq_ref, wkv_ref,
                      c_ref, sa_ref, sb_ref, q_ref, k_ref, v_ref, *, scale):
    xn = _rms(h_ref[...], gpre_ref[...]).astype(BF16)
    lat = _dot(xn, wd_ref[...])
    cq = _rms(lat[:, :Q_LORA], gq_ref[...]).astype(BF16)
    ckv = _rms(lat[:, Q_LORA:Q_LORA + KV_LORA], gkv_ref[...]).astype(BF16)
    c, sa, sb = c_ref[...], sa_ref[...], sb_ref[...]
    k_rope = _rope(lat[:, Q_LORA + KV_LORA:], c, sa, sb).astype(BF16)
    hv = A_HEADS * NOPE_DIM
    for hh in range(A_HEADS):
        qh = _dot(cq, wq_ref[:, hh * QK_PAD:(hh + 1) * QK_PAD])
        q_ref[:, hh * QK_PAD:hh * QK_PAD + LANES] = (qh[:, :LANES] * scale).astype(BF16)
        q_ref[:, hh * QK_PAD + LANES:(hh + 1) * QK_PAD] = (
            _rope(qh[:, LANES:], c, sa, sb) * scale).astype(BF16)
        kh = _dot(ckv, wkv_ref[:, hh * NOPE_DIM:(hh + 1) * NOPE_DIM])
        k_ref[:, hh * QK_PAD:hh * QK_PAD + LANES] = kh.astype(BF16)
        k_ref[:, hh * QK_PAD + LANES:(hh + 1) * QK_PAD] = k_rope
        vh = _dot(ckv, wkv_ref[:, hv + hh * V_DIM:hv + (hh + 1) * V_DIM])
        v_ref[:, hh * V_DIM:(hh + 1) * V_DIM] = vh.astype(BF16)


def mla_front(h, g_pre, wd, g_q, g_kv, wq, wkv, rope_c, rope_sa, rope_sb, *, seq, tm=256):
    t, d = h.shape
    nqk = A_HEADS * QK_PAD
    nv = A_HEADS * V_DIM
    pos_blocks = seq // tm
    const = lambda i: (0, 0)
    return pl.pallas_call(
        functools.partial(_mla_front_kernel, scale=(NOPE_DIM + ROPE_DIM) ** -0.5),
        grid=(t // tm,),
        in_specs=[
            pl.BlockSpec((tm, d), lambda i: (i, 0)),
            pl.BlockSpec((1, d), const),
            pl.BlockSpec((d, LAT_PAD), const),
            pl.BlockSpec((1, Q_LORA), const),
            pl.BlockSpec((1, KV_LORA), const),
            pl.BlockSpec((Q_LORA, nqk), const),
            pl.BlockSpec((KV_LORA, 2 * nv), const),
            pl.BlockSpec((tm, LANES), lambda i: (i % pos_blocks, 0)),
            pl.BlockSpec((tm, LANES), lambda i: (i % pos_blocks, 0)),
            pl.BlockSpec((tm, LANES), lambda i: (i % pos_blocks, 0)),
        ],
        out_specs=[
            pl.BlockSpec((tm, nqk), lambda i: (i, 0)),
            pl.BlockSpec((tm, nqk), lambda i: (i, 0)),
            pl.BlockSpec((tm, nv), lambda i: (i, 0)),
        ],
        out_shape=[
            jax.ShapeDtypeStruct((t, nqk), BF16),
            jax.ShapeDtypeStruct((t, nqk), BF16),
            jax.ShapeDtypeStruct((t, nv), BF16),
        ],
        compiler_params=_params("parallel"),
        name="mla_front",
    )(h, g_pre.reshape(1, d), wd, g_q.reshape(1, -1), g_kv.reshape(1, -1), wq, wkv,
      rope_c, rope_sa, rope_sb)


def _softmax_attn_kernel(q_ref, k_ref, v_ref, o_ref, m_ref, l_ref, acc_ref, *, hp, tq, dk, dv):
    seq = q_ref.shape[0]
    nq = seq // tq
    reps = tq // LANES
    row = lax.broadcasted_iota(jnp.int32, (tq, tq), 0)
    col = lax.broadcasted_iota(jnp.int32, (tq, tq), 1)
    causal = col <= row

    for hh in range(hp):
        def q_body(qi, carry, hh=hh):
            q0 = pl.multiple_of(qi * tq, tq)
            q = q_ref[pl.ds(q0, tq), hh * dk:(hh + 1) * dk]
            m_ref[...] = jnp.full(m_ref.shape, -jnp.inf, F32)
            l_ref[...] = jnp.zeros(l_ref.shape, F32)
            acc_ref[...] = jnp.zeros(acc_ref.shape, F32)

            def tile(k0, masked):
                k = k_ref[pl.ds(k0, tq), hh * dk:(hh + 1) * dk]
                v = v_ref[pl.ds(k0, tq), hh * dv:(hh + 1) * dv]
                s = _dot_nt(q, k)
                if masked:
                    s = jnp.where(causal, s, -jnp.inf)
                m_prev = m_ref[...]
                m_new = jnp.maximum(m_prev, jnp.max(s, axis=1, keepdims=True))
                p = jnp.exp(s - jnp.tile(m_new, (1, reps)))
                alpha = jnp.exp(m_prev - m_new)
                l_ref[...] = alpha * l_ref[...] + jnp.sum(p, axis=1, keepdims=True)
                acc_ref[...] = alpha * acc_ref[...] + _dot(p.astype(BF16), v)
                m_ref[...] = m_new

            def kv_body(ki, c):
                tile(pl.multiple_of(ki * tq, tq), False)
                return c

            lax.fori_loop(0, qi, kv_body, 0)
            tile(q0, True)
            o_ref[pl.ds(q0, tq), hh * dv:(hh + 1) * dv] = (acc_ref[...] / l_ref[...]).astype(BF16)
            return carry

        lax.fori_loop(0, nq, q_body, 0)


def softmax_attention(q, k, v, *, batch, seq, heads, dk, dv, hp=2, tq=256):
    t = batch * seq
    return pl.pallas_call(
        functools.partial(_softmax_attn_kernel, hp=hp, tq=tq, dk=dk, dv=dv),
        grid=(batch, heads // hp),
        in_specs=[
            pl.BlockSpec((seq, hp * dk), lambda b, g: (b, g)),
            pl.BlockSpec((seq, hp * dk), lambda b, g: (b, g)),
            pl.BlockSpec((seq, hp * dv), lambda b, g: (b, g)),
        ],
        out_specs=pl.BlockSpec((seq, hp * dv), lambda b, g: (b, g)),
        out_shape=jax.ShapeDtypeStruct((t, heads * dv), BF16),
        scratch_shapes=[
            pltpu.VMEM((tq, LANES), F32),
            pltpu.VMEM((tq, LANES), F32),
            pltpu.VMEM((tq, dv), F32),
        ],
        compiler_params=_params("parallel", "parallel"),
        name="softmax_attention",
    )(q, k, v)


def _stick_attn_kernel(q_ref, k_ref, v_ref, o_ref, r_ref, acc_ref, *, hp, tq, dh):
    seq = q_ref.shape[0]
    nq = seq // tq
    reps = tq // LANES
    row = lax.broadcasted_iota(jnp.int32, (tq, tq), 0)
    col = lax.broadcasted_iota(jnp.int32, (tq, tq), 1)
    strict = col < row
    tri = jnp.where(row >= col, 1.0, 0.0).astype(BF16)

    def suffix_sum(x):
        hi = x.astype(BF16)
        lo = (x - hi.astype(F32)).astype(BF16)
        return _dot(hi, tri) + _dot(lo, tri)

    for hh in range(hp):
        def q_body(qi, carry, hh=hh):
            q0 = pl.multiple_of(qi * tq, tq)
            q = q_ref[pl.ds(q0, tq), hh * dh:(hh + 1) * dh]

            def tile(k0, diagonal):
                k = k_ref[pl.ds(k0, tq), hh * dh:(hh + 1) * dh]
                v = v_ref[pl.ds(k0, tq), hh * dh:(hh + 1) * dh]
                z = _dot_nt(q, k)
                log_1m = jnp.minimum(-z, 0.0) - jnp.log(1.0 + jnp.exp(-jnp.abs(z)))
                if diagonal:
                    log_1m = jnp.where(strict, log_1m, 0.0)
                incl = suffix_sum(log_1m)
                if diagonal:
                    a = jnp.where(strict, jnp.exp(z + incl), 0.0)
                    acc_ref[...] = _dot(a.astype(BF16), v)
                    r_ref[...] = jnp.broadcast_to(incl[:, 0:1], r_ref.shape)
                else:
                    r = r_ref[...]
                    a = jnp.exp(z + incl + jnp.tile(r, (1, reps)))
                    acc_ref[...] += _dot(a.astype(BF16), v)
                    r_ref[...] = r + jnp.broadcast_to(incl[:, 0:1], r_ref.shape)

            tile(q0, True)

            def kv_body(step, c):
                tile(pl.multiple_of((qi - 1 - step) * tq, tq), False)
                return c

            lax.fori_loop(0, qi, kv_body, 0)
            o_ref[pl.ds(q0, tq), hh * dh:(hh + 1) * dh] = acc_ref[...].astype(BF16)
            return carry

        lax.fori_loop(0, nq, q_body, 0)


def stick_attention(q, kv, *, batch, seq, heads, dh, hp=2, tq=256):
    t = batch * seq
    v_off = heads // hp
    return pl.pallas_call(
        functools.partial(_stick_attn_kernel, hp=hp, tq=tq, dh=dh),
        grid=(batch, heads // hp),
        in_specs=[
            pl.BlockSpec((seq, hp * dh), lambda b, g: (b, g)),
            pl.BlockSpec((seq, hp * dh), lambda b, g: (b, g)),
            pl.BlockSpec((seq, hp * dh), lambda b, g: (b, v_off + g)),
        ],
        out_specs=pl.BlockSpec((seq, hp * dh), lambda b, g: (b, g)),
        out_shape=jax.ShapeDtypeStruct((t, heads * dh), BF16),
        scratch_shapes=[
            pltpu.VMEM((tq, LANES), F32),
            pltpu.VMEM((tq, dh), F32),
        ],
        compiler_params=_params("parallel", "parallel"),
        name="stick_attention",
    )(q, kv, kv)


def _proj_residual_kernel(o_ref, w_ref, g_ref, h_ref, out_ref):
    a = _dot(o_ref[...], w_ref[...])
    out_ref[...] = h_ref[...] + _rms(a, g_ref[...])


def proj_residual(o, w, g, h, *, tm=512):
    t, d = h.shape
    kdim = o.shape[1]
    return pl.pallas_call(
        _proj_residual_kernel,
        grid=(t // tm,),
        in_specs=[
            pl.BlockSpec((tm, kdim), lambda i: (i, 0)),
            pl.BlockSpec((kdim, d), lambda i: (0, 0)),
            pl.BlockSpec((1, d), lambda i: (0, 0)),
            pl.BlockSpec((tm, d), lambda i: (i, 0)),
        ],
        out_specs=pl.BlockSpec((tm, d), lambda i: (i, 0)),
        out_shape=jax.ShapeDtypeStruct((t, d), F32),
        compiler_params=_params("parallel"),
        name="proj_residual",
    )(o, w, g.reshape(1, d), h)


def _mlp_kernel(h_ref, gpre_ref, wup_ref, wdn_ref, gpost_ref, out_ref, xn_ref, acc_ref):
    f = pl.program_id(1)

    @pl.when(f == 0)
    def _():
        xn_ref[...] = _rms(h_ref[...], gpre_ref[...]).astype(BF16)
        acc_ref[...] = jnp.zeros(acc_ref.shape, F32)

    u = jnp.maximum(_dot(xn_ref[...], wup_ref[...]), 0.0)
    acc_ref[...] += _dot((u * u).astype(BF16), wdn_ref[...])

    @pl.when(f == pl.num_programs(1) - 1)
    def _():
        out_ref[...] = h_ref[...] + _rms(acc_ref[...], gpost_ref[...])


def mlp(h, g_pre, w_up, w_down, g_post, *, tm=512, tf=1024):
    t, d = h.shape
    ff = w_up.shape[1]
    return pl.pallas_call(
        _mlp_kernel,
        grid=(t // tm, ff // tf),
        in_specs=[
            pl.BlockSpec((tm, d), lambda i, f: (i, 0)),
            pl.BlockSpec((1, d), lambda i, f: (0, 0)),
            pl.BlockSpec((d, tf), lambda i, f: (0, f)),
            pl.BlockSpec((tf, d), lambda i, f: (f, 0)),
            pl.BlockSpec((1, d), lambda i, f: (0, 0)),
        ],
        out_specs=pl.BlockSpec((tm, d), lambda i, f: (i, 0)),
        out_shape=jax.ShapeDtypeStruct((t, d), F32),
        scratch_shapes=[pltpu.VMEM((tm, d), BF16), pltpu.VMEM((tm, d), F32)],
        compiler_params=_params("parallel", "arbitrary"),
        name="mlp",
    )(h, g_pre.reshape(1, d), w_up, w_down, g_post.reshape(1, d))


def _rope_tables(seq):
    pos = jnp.arange(seq, dtype=F32)
    inv_freq = ROPE_THETA ** (-jnp.arange(0, ROPE_DIM, 2, dtype=F32) / ROPE_DIM)
    ang = pos[:, None] * inv_freq[None, :]
    cos, sin = jnp.cos(ang), jnp.sin(ang)
    zero = jnp.zeros_like(cos)
    zero2 = jnp.zeros((seq, LANES - ROPE_DIM), F32)
    c = jnp.concatenate([cos, cos, zero2], axis=1)
    sa = jnp.concatenate([-sin, zero, zero2], axis=1)
    sb = jnp.concatenate([zero, sin, zero2], axis=1)
    return c, sa, sb


def _prep_mla_weights(w_dq_dkv, w_uq, w_ukv):
    d = w_dq_dkv.shape[0]
    wd = jnp.pad(w_dq_dkv, ((0, 0), (0, LAT_PAD - w_dq_dkv.shape[1]))).astype(BF16)
    wq = w_uq.reshape(Q_LORA, A_HEADS, NOPE_DIM + ROPE_DIM)
    wq = jnp.pad(wq, ((0, 0), (0, 0), (0, QK_PAD - NOPE_DIM - ROPE_DIM)))
    wq = wq.reshape(Q_LORA, A_HEADS * QK_PAD).astype(BF16)
    wkv = w_ukv.reshape(KV_LORA, A_HEADS, NOPE_DIM + V_DIM)
    wkv = jnp.concatenate(
        [wkv[:, :, :NOPE_DIM].reshape(KV_LORA, -1), wkv[:, :, NOPE_DIM:].reshape(KV_LORA, -1)],
        axis=1).astype(BF16)
    del d
    return wd, wq, wkv


def kernel(x, a_pre_g, a_post_g, a_w_dq_dkv, a_q_norm_g, a_kv_norm_g, a_w_uq, a_w_ukv, a_w_o,
           kv_norm_g, w_kv_shared, b_pre_g, b_post_g, b_w_q, b_w_o,
           mlp_pre_g, mlp_post_g, mlp_w_up, mlp_w_down):
    bsz, seq, d = x.shape
    n_a = a_pre_g.shape[0]
    n_b = b_pre_g.shape[0]
    h = x.reshape(bsz * seq, d)
    rope_c, rope_sa, rope_sb = _rope_tables(seq)
    kv_sh = None
    for layer in range(n_a + n_b):
        if layer < n_a:
            i = layer
            wd, wq, wkv = _prep_mla_weights(a_w_dq_dkv[i], a_w_uq[i], a_w_ukv[i])
            q, k, v = mla_front(h, a_pre_g[i], wd, a_q_norm_g[i], a_kv_norm_g[i], wq, wkv,
                                rope_c, rope_sa, rope_sb, seq=seq)
            o = softmax_attention(q, k, v, batch=bsz, seq=seq, heads=A_HEADS, dk=QK_PAD, dv=V_DIM)
            h = proj_residual(o, a_w_o[i].astype(BF16), a_post_g[i], h)
        else:
            j = layer - n_a
            q = norm_matmul(h, b_pre_g[j], b_w_q[j].astype(BF16), scale=B_HEAD_DIM ** -0.5)
            o = stick_attention(q, kv_sh, batch=bsz, seq=seq, heads=B_HEADS, dh=B_HEAD_DIM)
            h = proj_residual(o, b_w_o[j].astype(BF16), b_post_g[j], h)
        h = mlp(h, mlp_pre_g[layer], mlp_w_up[layer].astype(BF16), mlp_w_down[layer].astype(BF16),
                mlp_post_g[layer])
        if layer == n_a - 1:
            kv_sh = norm_matmul(h, kv_norm_g, w_kv_shared.astype(BF16))
    return h.reshape(bsz, seq, d)
```

```python
import functools

import jax
import jax.numpy as jnp
from jax import lax
from jax.experimental import pallas as pl
from jax.experimental.pallas import tpu as pltpu

D_MODEL = 2048
A_HEADS = 16
Q_LORA = 512
KV_LORA = 512
NOPE_DIM = 128
ROPE_DIM = 64
V_DIM = 128
ROPE_THETA = 10000.0
B_HEADS = 16
B_HEAD_DIM = 128
NORM_EPS = 1e-6

LANES = 128
QK_PAD = 256
LAT_PAD = Q_LORA + KV_LORA + LANES
VMEM_LIMIT = 56 * 1024 * 1024

BF16 = jnp.bfloat16
F32 = jnp.float32


def _params(*sem):
    return pltpu.CompilerParams(dimension_semantics=sem, vmem_limit_bytes=VMEM_LIMIT)


def _rms(x, g):
    return x * lax.rsqrt(jnp.mean(x * x, axis=-1, keepdims=True) + NORM_EPS) * g


def _dot(a, b):
    return jnp.dot(a, b, preferred_element_type=F32)


def _dot_nt(a, b):
    return lax.dot_general(a, b, (((1,), (1,)), ((), ())), preferred_element_type=F32)


def _rope(x, c, sa, sb):
    return x * c + pltpu.roll(x, 96, 1) * sa + pltpu.roll(x, 32, 1) * sb


def _norm_matmul_kernel(x_ref, g_ref, w_ref, o_ref, *, scale):
    y = _dot(_rms(x_ref[...], g_ref[...]).astype(BF16), w_ref[...])
    if scale != 1.0:
        y = y * scale
    o_ref[...] = y.astype(o_ref.dtype)


def norm_matmul(x, g, w, *, scale=1.0, tm=512, tn=2048):
    t, d = x.shape
    n = w.shape[1]
    return pl.pallas_call(
        functools.partial(_norm_matmul_kernel, scale=scale),
        grid=(n // tn, t // tm),
        in_specs=[
            pl.BlockSpec((tm, d), lambda j, i: (i, 0)),
            pl.BlockSpec((1, d), lambda j, i: (0, 0)),
            pl.BlockSpec((d, tn), lambda j, i: (0, j)),
        ],
        out_specs=pl.BlockSpec((tm, tn), lambda j, i: (i, j)),
        out_shape=jax.ShapeDtypeStruct((t, n), BF16),
        compiler_params=_params("parallel", "parallel"),
        name="norm_matmul",
    )(x, g.reshape(1, d), w)


def _mla_front_kernel(h_ref, gpre_ref, wd_ref, gq_ref, gkv_ref, wq_ref, wkv_ref,
                      c_ref, sa_ref, sb_ref, q_ref, k_ref, v_ref, *, scale):
    xn = _rms(h_ref[...], gpre_ref[...]).astype(BF16)
    lat = _dot(xn, wd_ref[...])
    cq = _rms(lat[:, :Q_LORA], gq_ref[...]).astype(BF16)
    ckv = _rms(lat[:, Q_LORA:Q_LORA + KV_LORA], gkv_ref[...]).astype(BF16)
    c, sa, sb = c_ref[...], sa_ref[...], sb_ref[...]
    k_rope = _rope(lat[:, Q_LORA + KV_LORA:], c, sa, sb).astype(BF16)
    hv = A_HEADS * NOPE_DIM
    for hh in range(A_HEADS):
        qh = _dot(cq, wq_ref[:, hh * QK_PAD:(hh + 1) * QK_PAD])
        q_ref[:, hh * QK_PAD:hh * QK_PAD + LANES] = (qh[:, :LANES] * scale).astype(BF16)
        q_ref[:, hh * QK_PAD + LANES:(hh + 1) * QK_PAD] = (
            _rope(qh[:, LANES:], c, sa, sb) * scale).astype(BF16)
        kh = _dot(ckv, wkv_ref[:, hh * NOPE_DIM:(hh + 1) * NOPE_DIM])
        k_ref[:, hh * QK_PAD:hh * QK_PAD + LANES] = kh.astype(BF16)
        k_ref[:, hh * QK_PAD + LANES:(hh + 1) * QK_PAD] = k_rope
        vh = _dot(ckv, wkv_ref[:, hv + hh * V_DIM:hv + (hh + 1) * V_DIM])
        v_ref[:, hh * V_DIM:(hh + 1) * V_DIM] = vh.astype(BF16)


def mla_front(h, g_pre, wd, g_q, g_kv, wq, wkv, rope_c, rope_sa, rope_sb, *, seq, tm=256):
    t, d = h.shape
    nqk = A_HEADS * QK_PAD
    nv = A_HEADS * V_DIM
    pos_blocks = seq // tm
    const = lambda i: (0, 0)
    return pl.pallas_call(
        functools.partial(_mla_front_kernel, scale=(NOPE_DIM + ROPE_DIM) ** -0.5),
        grid=(t // tm,),
        in_specs=[
            pl.BlockSpec((tm, d), lambda i: (i, 0)),
            pl.BlockSpec((1, d), const),
            pl.BlockSpec((d, LAT_PAD), const),
            pl.BlockSpec((1, Q_LORA), const),
            pl.BlockSpec((1, KV_LORA), const),
            pl.BlockSpec((Q_LORA, nqk), const),
            pl.BlockSpec((KV_LORA, 2 * nv), const),
            pl.BlockSpec((tm, LANES), lambda i: (i % pos_blocks, 0)),
            pl.BlockSpec((tm, LANES), lambda i: (i % pos_blocks, 0)),
            pl.BlockSpec((tm, LANES), lambda i: (i % pos_blocks, 0)),
        ],
        out_specs=[
            pl.BlockSpec((tm, nqk), lambda i: (i, 0)),
            pl.BlockSpec((tm, nqk), lambda i: (i, 0)),
            pl.BlockSpec((tm, nv), lambda i: (i, 0)),
        ],
        out_shape=[
            jax.ShapeDtypeStruct((t, nqk), BF16),
            jax.ShapeDtypeStruct((t, nqk), BF16),
            jax.ShapeDtypeStruct((t, nv), BF16),
        ],
        compiler_params=_params("parallel"),
        name="mla_front",
    )(h, g_pre.reshape(1, d), wd, g_q.reshape(1, -1), g_kv.reshape(1, -1), wq, wkv,
      rope_c, rope_sa, rope_sb)


def _softmax_attn_kernel(q_ref, k_ref, v_ref, o_ref, *, tq):
    seq = q_ref.shape[0]
    row = lax.broadcasted_iota(jnp.int32, (tq, tq), 0)
    col = lax.broadcasted_iota(jnp.int32, (tq, tq), 1)
    causal = col <= row

    for i in range(seq // tq):
        lo, hi = i * tq, (i + 1) * tq
        q = q_ref[lo:hi, :]
        s_d = jnp.where(causal, _dot_nt(q, k_ref[lo:hi, :]), -jnp.inf)
        m = jnp.max(s_d, axis=1, keepdims=True)
        if i > 0:
            s_o = _dot_nt(q, k_ref[0:lo, :])
            m = jnp.maximum(m, jnp.max(s_o, axis=1, keepdims=True))
        p_d = jnp.exp(s_d - m)
        l = jnp.sum(p_d, axis=1, keepdims=True)
        o = _dot(p_d.astype(BF16), v_ref[lo:hi, :])
        if i > 0:
            p_o = jnp.exp(s_o - m)
            l = l + jnp.sum(p_o, axis=1, keepdims=True)
            o = o + _dot(p_o.astype(BF16), v_ref[0:lo, :])
        o_ref[lo:hi, :] = (o / l).astype(BF16)


def softmax_attention(q, k, v, *, batch, seq, heads, dk, dv, tq=256):
    t = batch * seq
    return pl.pallas_call(
        functools.partial(_softmax_attn_kernel, tq=tq),
        grid=(batch, heads),
        in_specs=[
            pl.BlockSpec((seq, dk), lambda b, g: (b, g)),
            pl.BlockSpec((seq, dk), lambda b, g: (b, g)),
            pl.BlockSpec((seq, dv), lambda b, g: (b, g)),
        ],
        out_specs=pl.BlockSpec((seq, dv), lambda b, g: (b, g)),
        out_shape=jax.ShapeDtypeStruct((t, heads * dv), BF16),
        compiler_params=_params("parallel", "parallel"),
        name="softmax_attention",
    )(q, k, v)


STICK_DEAD = -110.0


def _log_1m_beta(z):
    return jnp.minimum(-z, 0.0) - jnp.log(1.0 + jnp.exp(-jnp.abs(z)))


def _stick_attn_kernel(q_ref, k_ref, v_ref, o_ref, r_ref, acc_ref, live_ref, *, tq):
    seq = q_ref.shape[0]
    nq = seq // tq
    reps = tq // LANES
    row = lax.broadcasted_iota(jnp.int32, (tq, tq), 0)
    col = lax.broadcasted_iota(jnp.int32, (tq, tq), 1)
    strict = col < row
    tri = jnp.where(row >= col, 1.0, 0.0).astype(BF16)

    def suffix_sum(x):
        hi = x.astype(BF16)
        lo = (x - hi.astype(F32)).astype(BF16)
        return _dot(hi, tri) + _dot(lo, tri)

    def lanes(x):
        return jnp.broadcast_to(x, (tq, LANES))

    def diagonal_block(q, lo):
        z = _dot_nt(q, k_ref[lo:lo + tq, :])
        incl = suffix_sum(jnp.where(strict, _log_1m_beta(z), 0.0))
        a = jnp.where(strict, jnp.exp(z + incl), 0.0)
        return _dot(a.astype(BF16), v_ref[lo:lo + tq, :]), lanes(incl[:, 0:1])

    def lower_block(q, k, v, r):
        z = _dot_nt(q, k)
        incl = suffix_sum(_log_1m_beta(z))
        a = jnp.exp(z + incl + jnp.tile(r, (1, reps)))
        return _dot(a.astype(BF16), v), r + lanes(incl[:, 0:1])

    for i in range(nq):
        lo = i * tq
        q = q_ref[lo:lo + tq, :]
        acc, r = diagonal_block(q, lo)
        if i > 0:
            da, r = lower_block(q, k_ref[lo - tq:lo, :], v_ref[lo - tq:lo, :], r)
            acc = acc + da
        o_ref[lo:lo + tq, :] = acc.astype(BF16)
        if i > 1:
            acc_ref[i] = acc
            r_ref[i] = r
            live_ref[i] = (jnp.max(r) > STICK_DEAD).astype(jnp.int32)

    for i in range(2, nq):
        @pl.when(live_ref[i] == 1)
        def _(i=i):
            lo = i * tq
            q = q_ref[lo:lo + tq, :]

            def body(step, c):
                k0 = pl.multiple_of((i - 2 - step) * tq, tq)
                da, r = lower_block(q, k_ref[pl.ds(k0, tq), :], v_ref[pl.ds(k0, tq), :], r_ref[i])
                acc_ref[i] += da
                r_ref[i] = r
                return c

            lax.fori_loop(0, i - 1, body, 0)
            o_ref[lo:lo + tq, :] = acc_ref[i].astype(BF16)


def stick_attention(q, kv, *, batch, seq, heads, dh, tq=256):
    t = batch * seq
    nq = seq // tq
    return pl.pallas_call(
        functools.partial(_stick_attn_kernel, tq=tq),
        grid=(batch, heads),
        in_specs=[
            pl.BlockSpec((seq, dh), lambda b, g: (b, g)),
            pl.BlockSpec((seq, dh), lambda b, g: (b, g)),
            pl.BlockSpec((seq, dh), lambda b, g: (b, heads + g)),
        ],
        out_specs=pl.BlockSpec((seq, dh), lambda b, g: (b, g)),
        out_shape=jax.ShapeDtypeStruct((t, heads * dh), BF16),
        scratch_shapes=[
            pltpu.VMEM((nq, tq, LANES), F32),
            pltpu.VMEM((nq, tq, dh), F32),
            pltpu.SMEM((nq,), jnp.int32),
        ],
        compiler_params=_params("parallel", "parallel"),
        name="stick_attention",
    )(q, kv, kv)


def _proj_residual_kernel(o_ref, w_ref, g_ref, h_ref, out_ref):
    a = _dot(o_ref[...], w_ref[...])
    out_ref[...] = h_ref[...] + _rms(a, g_ref[...])


def proj_residual(o, w, g, h, *, tm=512):
    t, d = h.shape
    kdim = o.shape[1]
    return pl.pallas_call(
        _proj_residual_kernel,
        grid=(t // tm,),
        in_specs=[
            pl.BlockSpec((tm, kdim), lambda i: (i, 0)),
            pl.BlockSpec((kdim, d), lambda i: (0, 0)),
            pl.BlockSpec((1, d), lambda i: (0, 0)),
            pl.BlockSpec((tm, d), lambda i: (i, 0)),
        ],
        out_specs=pl.BlockSpec((tm, d), lambda i: (i, 0)),
        out_shape=jax.ShapeDtypeStruct((t, d), F32),
        compiler_params=_params("parallel"),
        name="proj_residual",
    )(o, w, g.reshape(1, d), h)


def _mlp_kernel(h_ref, gpre_ref, wup_ref, wdn_ref, gpost_ref, out_ref, xn_ref, acc_ref):
    f = pl.program_id(1)

    @pl.when(f == 0)
    def _():
        xn_ref[...] = _rms(h_ref[...], gpre_ref[...]).astype(BF16)
        acc_ref[...] = jnp.zeros(acc_ref.shape, F32)

    u = jnp.maximum(_dot(xn_ref[...], wup_ref[...]), 0.0)
    acc_ref[...] += _dot((u * u).astype(BF16), wdn_ref[...])

    @pl.when(f == pl.num_programs(1) - 1)
    def _():
        out_ref[...] = h_ref[...] + _rms(acc_ref[...], gpost_ref[...])


def mlp(h, g_pre, w_up, w_down, g_post, *, tm=512, tf=1024):
    t, d = h.shape
    ff = w_up.shape[1]
    return pl.pallas_call(
        _mlp_kernel,
        grid=(t // tm, ff // tf),
        in_specs=[
            pl.BlockSpec((tm, d), lambda i, f: (i, 0)),
            pl.BlockSpec((1, d), lambda i, f: (0, 0)),
            pl.BlockSpec((d, tf), lambda i, f: (0, f)),
            pl.BlockSpec((tf, d), lambda i, f: (f, 0)),
            pl.BlockSpec((1, d), lambda i, f: (0, 0)),
        ],
        out_specs=pl.BlockSpec((tm, d), lambda i, f: (i, 0)),
        out_shape=jax.ShapeDtypeStruct((t, d), F32),
        scratch_shapes=[pltpu.VMEM((tm, d), BF16), pltpu.VMEM((tm, d), F32)],
        compiler_params=_params("parallel", "arbitrary"),
        name="mlp",
    )(h, g_pre.reshape(1, d), w_up, w_down, g_post.reshape(1, d))


def _rope_tables(seq):
    pos = jnp.arange(seq, dtype=F32)
    inv_freq = ROPE_THETA ** (-jnp.arange(0, ROPE_DIM, 2, dtype=F32) / ROPE_DIM)
    ang = pos[:, None] * inv_freq[None, :]
    cos, sin = jnp.cos(ang), jnp.sin(ang)
    zero = jnp.zeros_like(cos)
    zero2 = jnp.zeros((seq, LANES - ROPE_DIM), F32)
    c = jnp.concatenate([cos, cos, zero2], axis=1)
    sa = jnp.concatenate([-sin, zero, zero2], axis=1)
    sb = jnp.concatenate([zero, sin, zero2], axis=1)
    return c, sa, sb


def _prep_mla_weights(w_dq_dkv, w_uq, w_ukv):
    d = w_dq_dkv.shape[0]
    wd = jnp.pad(w_dq_dkv, ((0, 0), (0, LAT_PAD - w_dq_dkv.shape[1]))).astype(BF16)
    wq = w_uq.reshape(Q_LORA, A_HEADS, NOPE_DIM + ROPE_DIM)
    wq = jnp.pad(wq, ((0, 0), (0, 0), (0, QK_PAD - NOPE_DIM - ROPE_DIM)))
    wq = wq.reshape(Q_LORA, A_HEADS * QK_PAD).astype(BF16)
    wkv = w_ukv.reshape(KV_LORA, A_HEADS, NOPE_DIM + V_DIM)
    wkv = jnp.concatenate(
        [wkv[:, :, :NOPE_DIM].reshape(KV_LORA, -1), wkv[:, :, NOPE_DIM:].reshape(KV_LORA, -1)],
        axis=1).astype(BF16)
    del d
    return wd, wq, wkv


def kernel(x, a_pre_g, a_post_g, a_w_dq_dkv, a_q_norm_g, a_kv_norm_g, a_w_uq, a_w_ukv, a_w_o,
           kv_norm_g, w_kv_shared, b_pre_g, b_post_g, b_w_q, b_w_o,
           mlp_pre_g, mlp_post_g, mlp_w_up, mlp_w_down):
    bsz, seq, d = x.shape
    n_a = a_pre_g.shape[0]
    n_b = b_pre_g.shape[0]
    h = x.reshape(bsz * seq, d)
    rope_c, rope_sa, rope_sb = _rope_tables(seq)
    kv_sh = None
    for layer in range(n_a + n_b):
        if layer < n_a:
            i = layer
            wd, wq, wkv = _prep_mla_weights(a_w_dq_dkv[i], a_w_uq[i], a_w_ukv[i])
            q, k, v = mla_front(h, a_pre_g[i], wd, a_q_norm_g[i], a_kv_norm_g[i], wq, wkv,
                                rope_c, rope_sa, rope_sb, seq=seq)
            o = softmax_attention(q, k, v, batch=bsz, seq=seq, heads=A_HEADS, dk=QK_PAD, dv=V_DIM)
            h = proj_residual(o, a_w_o[i].astype(BF16), a_post_g[i], h)
        else:
            j = layer - n_a
            q = norm_matmul(h, b_pre_g[j], b_w_q[j].astype(BF16), scale=B_HEAD_DIM ** -0.5)
            o = stick_attention(q, kv_sh, batch=bsz, seq=seq, heads=B_HEADS, dh=B_HEAD_DIM)
            h = proj_residual(o, b_w_o[j].astype(BF16), b_post_g[j], h)
        h = mlp(h, mlp_pre_g[layer], mlp_w_up[layer].astype(BF16), mlp_w_down[layer].astype(BF16),
                mlp_post_g[layer])
        if layer == n_a - 1:
            kv_sh = norm_matmul(h, kv_norm_g, w_kv_shared.astype(BF16))
    return h.reshape(bsz, seq, d)
```

```python
import functools

import jax
import jax.numpy as jnp
from jax import lax
from jax.experimental import pallas as pl
from jax.experimental.pallas import tpu as pltpu

D_MODEL = 2048
A_HEADS = 16
Q_LORA = 512
KV_LORA = 512
NOPE_DIM = 128
ROPE_DIM = 64
V_DIM = 128
ROPE_THETA = 10000.0
B_HEADS = 16
B_HEAD_DIM = 128
NORM_EPS = 1e-6
LOG2E = 1.4426950408889634

LANES = 128
QK_PAD = 256
LAT_PAD = Q_LORA + KV_LORA + LANES
VMEM_LIMIT = 56 * 1024 * 1024

BF16 = jnp.bfloat16
F32 = jnp.float32


def _params(*sem):
    return pltpu.CompilerParams(dimension_semantics=sem, vmem_limit_bytes=VMEM_LIMIT)


def _rms(x, g):
    return x * lax.rsqrt(jnp.mean(x * x, axis=-1, keepdims=True) + NORM_EPS) * g


def _dot(a, b):
    return jnp.dot(a, b, preferred_element_type=F32)


def _dot_nt(a, b):
    return lax.dot_general(a, b, (((1,), (1,)), ((), ())), preferred_element_type=F32)


def _gain_spec(n, li):
    return pl.BlockSpec((None, 1, n), lambda *_: (li, 0, 0))


def _rope(x, c, sa, sb):
    return x * c + pltpu.roll(x, 96, 1) * sa + pltpu.roll(x, 32, 1) * sb


def _norm_matmul_kernel(x_ref, g_ref, w_ref, o_ref, *, scale):
    y = _dot(_rms(x_ref[...], g_ref[...]).astype(BF16), w_ref[...])
    if scale != 1.0:
        y = y * scale
    o_ref[...] = y.astype(o_ref.dtype)


def norm_matmul(x, g, w, li, *, scale=1.0, tm=512, tn=2048):
    t, d = x.shape
    n = w.shape[2]
    return pl.pallas_call(
        functools.partial(_norm_matmul_kernel, scale=scale),
        grid=(n // tn, t // tm),
        in_specs=[
            pl.BlockSpec((tm, d), lambda j, i: (i, 0)),
            _gain_spec(d, li),
            pl.BlockSpec((None, d, tn), lambda j, i: (li, 0, j)),
        ],
        out_specs=pl.BlockSpec((tm, tn), lambda j, i: (i, j)),
        out_shape=jax.ShapeDtypeStruct((t, n), BF16),
        compiler_params=_params("parallel", "parallel"),
        name="norm_matmul",
    )(x, g.reshape(-1, 1, d), w)


def _mla_front_kernel(h_ref, gpre_ref, wd_ref, gq_ref, gkv_ref, wq_ref, wkv_ref,
                      c_ref, sa_ref, sb_ref, q_ref, k_ref, v_ref, *, scale):
    xn = _rms(h_ref[...], gpre_ref[...]).astype(BF16)
    lat = _dot(xn, wd_ref[...])
    cq = _rms(lat[:, :Q_LORA], gq_ref[...]).astype(BF16)
    ckv = _rms(lat[:, Q_LORA:Q_LORA + KV_LORA], gkv_ref[...]).astype(BF16)
    c, sa, sb = c_ref[...], sa_ref[...], sb_ref[...]
    k_rope = _rope(lat[:, Q_LORA + KV_LORA:], c, sa, sb).astype(BF16)
    hv = A_HEADS * NOPE_DIM
    for hh in range(A_HEADS):
        qh = _dot(cq, wq_ref[:, hh * QK_PAD:(hh + 1) * QK_PAD])
        q_ref[:, hh * QK_PAD:hh * QK_PAD + LANES] = (qh[:, :LANES] * scale).astype(BF16)
        q_ref[:, hh * QK_PAD + LANES:(hh + 1) * QK_PAD] = (
            _rope(qh[:, LANES:], c, sa, sb) * scale).astype(BF16)
        kh = _dot(ckv, wkv_ref[:, hh * NOPE_DIM:(hh + 1) * NOPE_DIM])
        k_ref[:, hh * QK_PAD:hh * QK_PAD + LANES] = kh.astype(BF16)
        k_ref[:, hh * QK_PAD + LANES:(hh + 1) * QK_PAD] = k_rope
        vh = _dot(ckv, wkv_ref[:, hv + hh * V_DIM:hv + (hh + 1) * V_DIM])
        v_ref[:, hh * V_DIM:(hh + 1) * V_DIM] = vh.astype(BF16)


def mla_front(h, g_pre, wd, g_q, g_kv, wq, wkv, li, rope_c, rope_sa, rope_sb, *, seq, tm=256):
    t, d = h.shape
    nqk = A_HEADS * QK_PAD
    nv = A_HEADS * V_DIM
    pos_blocks = seq // tm
    layer = lambda i: (li, 0, 0)
    return pl.pallas_call(
        functools.partial(_mla_front_kernel, scale=LOG2E * (NOPE_DIM + ROPE_DIM) ** -0.5),
        grid=(t // tm,),
        in_specs=[
            pl.BlockSpec((tm, d), lambda i: (i, 0)),
            _gain_spec(d, li),
            pl.BlockSpec((None, d, LAT_PAD), layer),
            _gain_spec(Q_LORA, li),
            _gain_spec(KV_LORA, li),
            pl.BlockSpec((None, Q_LORA, nqk), layer),
            pl.BlockSpec((None, KV_LORA, 2 * nv), layer),
            pl.BlockSpec((tm, LANES), lambda i: (i % pos_blocks, 0)),
            pl.BlockSpec((tm, LANES), lambda i: (i % pos_blocks, 0)),
            pl.BlockSpec((tm, LANES), lambda i: (i % pos_blocks, 0)),
        ],
        out_specs=[
            pl.BlockSpec((tm, nqk), lambda i: (i, 0)),
            pl.BlockSpec((tm, nqk), lambda i: (i, 0)),
            pl.BlockSpec((tm, nv), lambda i: (i, 0)),
        ],
        out_shape=[
            jax.ShapeDtypeStruct((t, nqk), BF16),
            jax.ShapeDtypeStruct((t, nqk), BF16),
            jax.ShapeDtypeStruct((t, nv), BF16),
        ],
        compiler_params=_params("parallel"),
        name="mla_front",
    )(h, g_pre.reshape(-1, 1, d), wd, g_q.reshape(-1, 1, Q_LORA), g_kv.reshape(-1, 1, KV_LORA),
      wq, wkv, rope_c, rope_sa, rope_sb)


def _softmax_attn_kernel(q_ref, k_ref, v_ref, o_ref, *, tq):
    seq = q_ref.shape[0]
    row = lax.broadcasted_iota(jnp.int32, (tq, tq), 0)
    col = lax.broadcasted_iota(jnp.int32, (tq, tq), 1)
    causal = col <= row

    for i in range(seq // tq):
        lo, hi = i * tq, (i + 1) * tq
        q = q_ref[lo:hi, :]
        s_d = jnp.where(causal, _dot_nt(q, k_ref[lo:hi, :]), -jnp.inf)
        m = jnp.max(s_d, axis=1, keepdims=True)
        if i > 0:
            s_o = _dot_nt(q, k_ref[0:lo, :])
            m = jnp.maximum(m, jnp.max(s_o, axis=1, keepdims=True))
        p_d = jnp.exp2(s_d - m)
        l = jnp.sum(p_d, axis=1, keepdims=True)
        o = _dot(p_d.astype(BF16), v_ref[lo:hi, :])
        if i > 0:
            p_o = jnp.exp2(s_o - m)
            l = l + jnp.sum(p_o, axis=1, keepdims=True)
            o = o + _dot(p_o.astype(BF16), v_ref[0:lo, :])
        o_ref[lo:hi, :] = (o / l).astype(BF16)


def softmax_attention(q, k, v, *, batch, seq, heads, dk, dv, tq=256):
    t = batch * seq
    return pl.pallas_call(
        functools.partial(_softmax_attn_kernel, tq=tq),
        grid=(batch, heads),
        in_specs=[
            pl.BlockSpec((seq, dk), lambda b, g: (b, g)),
            pl.BlockSpec((seq, dk), lambda b, g: (b, g)),
            pl.BlockSpec((seq, dv), lambda b, g: (b, g)),
        ],
        out_specs=pl.BlockSpec((seq, dv), lambda b, g: (b, g)),
        out_shape=jax.ShapeDtypeStruct((t, heads * dv), BF16),
        compiler_params=_params("parallel", "parallel"),
        name="softmax_attention",
    )(q, k, v)


def _softplus2(t):
    return jnp.maximum(t, 0.0) + jnp.log(1.0 + jnp.exp2(-jnp.abs(t))) * LOG2E


def _stick_attn_kernel(q_ref, k_ref, v_ref, o_ref, *, tq):
    seq = q_ref.shape[0]
    reps = tq // LANES
    row = lax.broadcasted_iota(jnp.int32, (tq, tq), 0)
    col = lax.broadcasted_iota(jnp.int32, (tq, tq), 1)
    strict = col < row
    above = jnp.where(row > col, 1.0, 0.0).astype(BF16)

    for i in range(seq // tq):
        lo, hi = i * tq, (i + 1) * tq
        z = _dot_nt(q_ref[lo:hi, :], k_ref[0:hi, :])
        a_blocks = [None] * (i + 1)
        passed = None
        for b in range(i, -1, -1):
            zb = z[:, b * tq:(b + 1) * tq]
            sp = _softplus2(zb)
            if b == i:
                sp = jnp.where(strict, sp, 0.0)
            sp_bf = sp.astype(BF16)
            excl = _dot(sp_bf, above)
            log_a = (zb - sp) - excl
            if passed is not None:
                log_a = log_a - jnp.tile(passed, (1, reps))
            a = jnp.exp2(log_a)
            if b == i:
                a = jnp.where(strict, a, 0.0)
            a_blocks[b] = a.astype(BF16)
            total = jnp.broadcast_to(excl[:, 0:1] + sp_bf[:, 0:1].astype(F32), (tq, LANES))
            passed = total if passed is None else passed + total
        a_all = a_blocks[0] if i == 0 else jnp.concatenate(a_blocks, axis=1)
        o_ref[lo:hi, :] = _dot(a_all, v_ref[0:hi, :]).astype(BF16)


def stick_attention(q, kv, *, batch, seq, heads, dh, tq=256):
    t = batch * seq
    return pl.pallas_call(
        functools.partial(_stick_attn_kernel, tq=tq),
        grid=(batch, heads),
        in_specs=[
            pl.BlockSpec((seq, dh), lambda b, g: (b, g)),
            pl.BlockSpec((seq, dh), lambda b, g: (b, g)),
            pl.BlockSpec((seq, dh), lambda b, g: (b, heads + g)),
        ],
        out_specs=pl.BlockSpec((seq, dh), lambda b, g: (b, g)),
        out_shape=jax.ShapeDtypeStruct((t, heads * dh), BF16),
        compiler_params=_params("parallel", "parallel"),
        name="stick_attention",
    )(q, kv, kv)


def _proj_residual_kernel(o_ref, w_ref, g_ref, h_ref, out_ref):
    a = _dot(o_ref[...], w_ref[...])
    out_ref[...] = h_ref[...] + _rms(a, g_ref[...])


def proj_residual(o, w, g, li, h, *, tm=512):
    t, d = h.shape
    kdim = o.shape[1]
    return pl.pallas_call(
        _proj_residual_kernel,
        grid=(t // tm,),
        in_specs=[
            pl.BlockSpec((tm, kdim), lambda i: (i, 0)),
            pl.BlockSpec((None, kdim, d), lambda i: (li, 0, 0)),
            _gain_spec(d, li),
            pl.BlockSpec((tm, d), lambda i: (i, 0)),
        ],
        out_specs=pl.BlockSpec((tm, d), lambda i: (i, 0)),
        out_shape=jax.ShapeDtypeStruct((t, d), F32),
        compiler_params=_params("parallel"),
        name="proj_residual",
    )(o, w, g.reshape(-1, 1, d), h)


def _mlp_kernel(h_ref, gpre_ref, wup_ref, wdn_ref, gpost_ref, out_ref, xn_ref, acc_ref):
    f = pl.program_id(1)

    @pl.when(f == 0)
    def _():
        xn_ref[...] = _rms(h_ref[...], gpre_ref[...]).astype(BF16)
        acc_ref[...] = jnp.zeros(acc_ref.shape, F32)

    u = jnp.maximum(_dot(xn_ref[...], wup_ref[...]), 0.0)
    acc_ref[...] += _dot((u * u).astype(BF16), wdn_ref[...])

    @pl.when(f == pl.num_programs(1) - 1)
    def _():
        out_ref[...] = h_ref[...] + _rms(acc_ref[...], gpost_ref[...])


def mlp(h, g_pre, w_up, w_down, g_post, li, *, tm=512, tf=1024):
    t, d = h.shape
    ff = w_up.shape[2]
    return pl.pallas_call(
        _mlp_kernel,
        grid=(t // tm, ff // tf),
        in_specs=[
            pl.BlockSpec((tm, d), lambda i, f: (i, 0)),
            _gain_spec(d, li),
            pl.BlockSpec((None, d, tf), lambda i, f: (li, 0, f)),
            pl.BlockSpec((None, tf, d), lambda i, f: (li, f, 0)),
            _gain_spec(d, li),
        ],
        out_specs=pl.BlockSpec((tm, d), lambda i, f: (i, 0)),
        out_shape=jax.ShapeDtypeStruct((t, d), F32),
        scratch_shapes=[pltpu.VMEM((tm, d), BF16), pltpu.VMEM((tm, d), F32)],
        compiler_params=_params("parallel", "arbitrary"),
        name="mlp",
    )(h, g_pre.reshape(-1, 1, d), w_up, w_down, g_post.reshape(-1, 1, d))


def _rope_tables(seq):
    pos = jnp.arange(seq, dtype=F32)
    inv_freq = ROPE_THETA ** (-jnp.arange(0, ROPE_DIM, 2, dtype=F32) / ROPE_DIM)
    ang = pos[:, None] * inv_freq[None, :]
    cos, sin = jnp.cos(ang), jnp.sin(ang)
    zero = jnp.zeros_like(cos)
    zero2 = jnp.zeros((seq, LANES - ROPE_DIM), F32)
    c = jnp.concatenate([cos, cos, zero2], axis=1)
    sa = jnp.concatenate([-sin, zero, zero2], axis=1)
    sb = jnp.concatenate([zero, sin, zero2], axis=1)
    return c, sa, sb


def _prep_mla_weights(w_dq_dkv, w_uq, w_ukv):
    nl = w_dq_dkv.shape[0]
    wd = jnp.pad(w_dq_dkv, ((0, 0), (0, 0), (0, LAT_PAD - w_dq_dkv.shape[2]))).astype(BF16)
    wq = w_uq.reshape(nl, Q_LORA, A_HEADS, NOPE_DIM + ROPE_DIM)
    wq = jnp.pad(wq, ((0, 0), (0, 0), (0, 0), (0, QK_PAD - NOPE_DIM - ROPE_DIM)))
    wq = wq.reshape(nl, Q_LORA, A_HEADS * QK_PAD).astype(BF16)
    wkv = w_ukv.reshape(nl, KV_LORA, A_HEADS, NOPE_DIM + V_DIM)
    wkv = jnp.concatenate(
        [wkv[..., :NOPE_DIM].reshape(nl, KV_LORA, -1), wkv[..., NOPE_DIM:].reshape(nl, KV_LORA, -1)],
        axis=2).astype(BF16)
    return wd, wq, wkv


def kernel(x, a_pre_g, a_post_g, a_w_dq_dkv, a_q_norm_g, a_kv_norm_g, a_w_uq, a_w_ukv, a_w_o,
           kv_norm_g, w_kv_shared, b_pre_g, b_post_g, b_w_q, b_w_o,
           mlp_pre_g, mlp_post_g, mlp_w_up, mlp_w_down):
    bsz, seq, d = x.shape
    n_a = a_pre_g.shape[0]
    n_b = b_pre_g.shape[0]
    h = x.reshape(bsz * seq, d)
    rope_c, rope_sa, rope_sb = _rope_tables(seq)
    wd, wq, wkv = _prep_mla_weights(a_w_dq_dkv, a_w_uq, a_w_ukv)
    a_wo, b_wq, b_wo = a_w_o.astype(BF16), b_w_q.astype(BF16), b_w_o.astype(BF16)
    w_up, w_down = mlp_w_up.astype(BF16), mlp_w_down.astype(BF16)
    w_kv = w_kv_shared.astype(BF16)[None]
    kv_sh = None
    for layer in range(n_a + n_b):
        if layer < n_a:
            i = layer
            q, k, v = mla_front(h, a_pre_g, wd, a_q_norm_g, a_kv_norm_g, wq, wkv, i,
                                rope_c, rope_sa, rope_sb, seq=seq)
            o = softmax_attention(q, k, v, batch=bsz, seq=seq, heads=A_HEADS, dk=QK_PAD, dv=V_DIM)
            h = proj_residual(o, a_wo, a_post_g, i, h)
        else:
            j = layer - n_a
            q = norm_matmul(h, b_pre_g, b_wq, j, scale=LOG2E * B_HEAD_DIM ** -0.5)
            o = stick_attention(q, kv_sh, batch=bsz, seq=seq, heads=B_HEADS, dh=B_HEAD_DIM)
            h = proj_residual(o, b_wo, b_post_g, j, h)
        h = mlp(h, mlp_pre_g, w_up, w_down, mlp_post_g, layer)
        if layer == n_a - 1:
            kv_sh = norm_matmul(h, kv_norm_g[None], w_kv, 0)
    return h.reshape(bsz, seq, d)
```

```python
import functools

import jax
import jax.numpy as jnp
from jax import lax
from jax.experimental import pallas as pl
from jax.experimental.pallas import tpu as pltpu

D_MODEL = 2048
A_HEADS = 16
Q_LORA = 512
KV_LORA = 512
NOPE_DIM = 128
ROPE_DIM = 64
V_DIM = 128
ROPE_THETA = 10000.0
B_HEADS = 16
B_HEAD_DIM = 128
NORM_EPS = 1e-6
LOG2E = 1.4426950408889634

LANES = 128
QK_PAD = 256
LAT_PAD = Q_LORA + KV_LORA + LANES
VMEM_LIMIT = 56 * 1024 * 1024

BF16 = jnp.bfloat16
F32 = jnp.float32


def _params(*sem):
    return pltpu.CompilerParams(dimension_semantics=sem, vmem_limit_bytes=VMEM_LIMIT)


def _rms(x, g):
    return x * lax.rsqrt(jnp.mean(x * x, axis=-1, keepdims=True) + NORM_EPS) * g


def _dot(a, b):
    return jnp.dot(a, b, preferred_element_type=F32)


def _dot_nt(a, b):
    return lax.dot_general(a, b, (((1,), (1,)), ((), ())), preferred_element_type=F32)


def _gain_spec(n, li):
    return pl.BlockSpec((None, 1, n), lambda *_: (li, 0, 0))


def _rope(x, c, sa, sb):
    return x * c + pltpu.roll(x, 96, 1) * sa + pltpu.roll(x, 32, 1) * sb


def _norm_matmul_kernel(x_ref, g_ref, w_ref, o_ref, *, scale):
    y = _dot(_rms(x_ref[...], g_ref[...]).astype(BF16), w_ref[...])
    if scale != 1.0:
        y = y * scale
    o_ref[...] = y.astype(o_ref.dtype)


def norm_matmul(x, g, w, li, *, scale=1.0, tm=512, tn=2048):
    t, d = x.shape
    n = w.shape[2]
    return pl.pallas_call(
        functools.partial(_norm_matmul_kernel, scale=scale),
        grid=(n // tn, t // tm),
        in_specs=[
            pl.BlockSpec((tm, d), lambda j, i: (i, 0)),
            _gain_spec(d, li),
            pl.BlockSpec((None, d, tn), lambda j, i: (li, 0, j)),
        ],
        out_specs=pl.BlockSpec((tm, tn), lambda j, i: (i, j)),
        out_shape=jax.ShapeDtypeStruct((t, n), BF16),
        compiler_params=_params("parallel", "parallel"),
        name="norm_matmul",
    )(x, g.reshape(-1, 1, d), w)


def _mla_front_kernel(h_ref, gpre_ref, wd_ref, gq_ref, gkv_ref, wq_ref, wkv_ref,
                      c_ref, sa_ref, sb_ref, q_ref, k_ref, v_ref, *, scale):
    xn = _rms(h_ref[...], gpre_ref[...]).astype(BF16)
    lat = _dot(xn, wd_ref[...])
    cq = _rms(lat[:, :Q_LORA], gq_ref[...]).astype(BF16)
    ckv = _rms(lat[:, Q_LORA:Q_LORA + KV_LORA], gkv_ref[...]).astype(BF16)
    c, sa, sb = c_ref[...], sa_ref[...], sb_ref[...]
    k_rope = _rope(lat[:, Q_LORA + KV_LORA:], c, sa, sb).astype(BF16)
    hv = A_HEADS * NOPE_DIM
    for hh in range(A_HEADS):
        qh = _dot(cq, wq_ref[:, hh * QK_PAD:(hh + 1) * QK_PAD])
        q_ref[:, hh * QK_PAD:hh * QK_PAD + LANES] = (qh[:, :LANES] * scale).astype(BF16)
        q_ref[:, hh * QK_PAD + LANES:(hh + 1) * QK_PAD] = (
            _rope(qh[:, LANES:], c, sa, sb) * scale).astype(BF16)
        k_ref[:, hh * QK_PAD + LANES:(hh + 1) * QK_PAD] = k_rope
    for hh in range(0, A_HEADS, 2):
        kh = _dot(ckv, wkv_ref[:, hh * NOPE_DIM:(hh + 2) * NOPE_DIM]).astype(BF16)
        k_ref[:, hh * QK_PAD:hh * QK_PAD + LANES] = kh[:, :LANES]
        k_ref[:, (hh + 1) * QK_PAD:(hh + 1) * QK_PAD + LANES] = kh[:, LANES:]
        vh = _dot(ckv, wkv_ref[:, hv + hh * V_DIM:hv + (hh + 2) * V_DIM])
        v_ref[:, hh * V_DIM:(hh + 2) * V_DIM] = vh.astype(BF16)


def mla_front(h, g_pre, wd, g_q, g_kv, wq, wkv, li, rope_c, rope_sa, rope_sb, *, seq, tm=512):
    t, d = h.shape
    nqk = A_HEADS * QK_PAD
    nv = A_HEADS * V_DIM
    pos_blocks = seq // tm
    layer = lambda i: (li, 0, 0)
    return pl.pallas_call(
        functools.partial(_mla_front_kernel, scale=LOG2E * (NOPE_DIM + ROPE_DIM) ** -0.5),
        grid=(t // tm,),
        in_specs=[
            pl.BlockSpec((tm, d), lambda i: (i, 0)),
            _gain_spec(d, li),
            pl.BlockSpec((None, d, LAT_PAD), layer, pipeline_mode=pl.Buffered(1)),
            _gain_spec(Q_LORA, li),
            _gain_spec(KV_LORA, li),
            pl.BlockSpec((None, Q_LORA, nqk), layer, pipeline_mode=pl.Buffered(1)),
            pl.BlockSpec((None, KV_LORA, 2 * nv), layer, pipeline_mode=pl.Buffered(1)),
            pl.BlockSpec((tm, LANES), lambda i: (i % pos_blocks, 0)),
            pl.BlockSpec((tm, LANES), lambda i: (i % pos_blocks, 0)),
            pl.BlockSpec((tm, LANES), lambda i: (i % pos_blocks, 0)),
        ],
        out_specs=[
            pl.BlockSpec((tm, nqk), lambda i: (i, 0)),
            pl.BlockSpec((tm, nqk), lambda i: (i, 0)),
            pl.BlockSpec((tm, nv), lambda i: (i, 0)),
        ],
        out_shape=[
            jax.ShapeDtypeStruct((t, nqk), BF16),
            jax.ShapeDtypeStruct((t, nqk), BF16),
            jax.ShapeDtypeStruct((t, nv), BF16),
        ],
        compiler_params=_params("parallel"),
        name="mla_front",
    )(h, g_pre.reshape(-1, 1, d), wd, g_q.reshape(-1, 1, Q_LORA), g_kv.reshape(-1, 1, KV_LORA),
      wq, wkv, rope_c, rope_sa, rope_sb)


def _softmax_attn_kernel(q_ref, k_ref, v_ref, o_ref, *, tq, hp, dk, dv):
    seq = q_ref.shape[0]
    row = lax.broadcasted_iota(jnp.int32, (tq, tq), 0)
    col = lax.broadcasted_iota(jnp.int32, (tq, tq), 1)
    causal = col <= row

    def scores(hh, i):
        lo, hi = i * tq, (i + 1) * tq
        kc = slice(hh * dk, (hh + 1) * dk)
        q = q_ref[lo:hi, kc]
        s_d = jnp.where(causal, _dot_nt(q, k_ref[lo:hi, kc]), -jnp.inf)
        m = jnp.max(s_d, axis=1, keepdims=True)
        s_o = None
        if i > 0:
            s_o = _dot_nt(q, k_ref[0:lo, kc])
            m = jnp.maximum(m, jnp.max(s_o, axis=1, keepdims=True))
        return s_d, s_o, m

    def weights_times_values(hh, i, s_d, s_o, m):
        lo, hi = i * tq, (i + 1) * tq
        vc = slice(hh * dv, (hh + 1) * dv)
        p_d = jnp.exp2(s_d - m)
        l = jnp.sum(p_d, axis=1, keepdims=True)
        o = _dot(p_d.astype(BF16), v_ref[lo:hi, vc])
        if i > 0:
            p_o = jnp.exp2(s_o - m)
            l = l + jnp.sum(p_o, axis=1, keepdims=True)
            o = o + _dot(p_o.astype(BF16), v_ref[0:lo, vc])
        o_ref[lo:hi, vc] = (o / l).astype(BF16)

    tiles = [(hh, i) for hh in range(hp) for i in range(seq // tq)]
    pending = scores(*tiles[0])
    for n, tile in enumerate(tiles):
        nxt = scores(*tiles[n + 1]) if n + 1 < len(tiles) else None
        weights_times_values(*tile, *pending)
        pending = nxt


def softmax_attention(q, k, v, *, batch, seq, heads, dk, dv, tq=256, hp=2):
    t = batch * seq
    return pl.pallas_call(
        functools.partial(_softmax_attn_kernel, tq=tq, hp=hp, dk=dk, dv=dv),
        grid=(batch, heads // hp),
        in_specs=[
            pl.BlockSpec((seq, hp * dk), lambda b, g: (b, g)),
            pl.BlockSpec((seq, hp * dk), lambda b, g: (b, g)),
            pl.BlockSpec((seq, hp * dv), lambda b, g: (b, g)),
        ],
        out_specs=pl.BlockSpec((seq, hp * dv), lambda b, g: (b, g)),
        out_shape=jax.ShapeDtypeStruct((t, heads * dv), BF16),
        compiler_params=_params("parallel", "parallel"),
        name="softmax_attention",
    )(q, k, v)


def _softplus2(t):
    return jnp.maximum(t, 0.0) + jnp.log(1.0 + jnp.exp2(-jnp.abs(t))) * LOG2E


def _stick_attn_kernel(q_ref, k_ref, v_ref, o_ref, *, tq, hp, dh):
    seq = q_ref.shape[0]
    reps = tq // LANES
    row = lax.broadcasted_iota(jnp.int32, (tq, tq), 0)
    col = lax.broadcasted_iota(jnp.int32, (tq, tq), 1)
    strict = col < row
    above = jnp.where(row > col, 1.0, 0.0).astype(BF16)

    def scores(hh, i):
        lo, hi = i * tq, (i + 1) * tq
        hc = slice(hh * dh, (hh + 1) * dh)
        z = _dot_nt(q_ref[lo:hi, hc], k_ref[0:hi, hc])
        parts = [None] * (i + 1)
        passed = None
        for b in range(i, -1, -1):
            zb = z[:, b * tq:(b + 1) * tq]
            sp = _softplus2(zb)
            if b == i:
                sp = jnp.where(strict, sp, 0.0)
            sp_bf = sp.astype(BF16)
            excl = _dot(sp_bf, above)
            parts[b] = (zb - sp, excl, passed)
            total = jnp.broadcast_to(excl[:, 0:1] + sp_bf[:, 0:1].astype(F32), (tq, LANES))
            passed = total if passed is None else passed + total
        return parts

    def weights_times_values(hh, i, parts):
        lo, hi = i * tq, (i + 1) * tq
        hc = slice(hh * dh, (hh + 1) * dh)
        a_blocks = []
        for b, (log_beta, excl, passed) in enumerate(parts):
            log_a = log_beta - excl
            if passed is not None:
                log_a = log_a - jnp.tile(passed, (1, reps))
            a = jnp.exp2(log_a)
            if b == i:
                a = jnp.where(strict, a, 0.0)
            a_blocks.append(a.astype(BF16))
        a_all = a_blocks[0] if i == 0 else jnp.concatenate(a_blocks, axis=1)
        o_ref[lo:hi, hc] = _dot(a_all, v_ref[0:hi, hc]).astype(BF16)

    tiles = [(hh, i) for hh in range(hp) for i in range(seq // tq)]
    pending = scores(*tiles[0])
    for n, tile in enumerate(tiles):
        nxt = scores(*tiles[n + 1]) if n + 1 < len(tiles) else None
        weights_times_values(*tile, pending)
        pending = nxt


def stick_attention(q, kv, *, batch, seq, heads, dh, tq=256, hp=2):
    t = batch * seq
    v_off = heads // hp
    return pl.pallas_call(
        functools.partial(_stick_attn_kernel, tq=tq, hp=hp, dh=dh),
        grid=(batch, heads // hp),
        in_specs=[
            pl.BlockSpec((seq, hp * dh), lambda b, g: (b, g)),
            pl.BlockSpec((seq, hp * dh), lambda b, g: (b, g)),
            pl.BlockSpec((seq, hp * dh), lambda b, g: (b, v_off + g)),
        ],
        out_specs=pl.BlockSpec((seq, hp * dh), lambda b, g: (b, g)),
        out_shape=jax.ShapeDtypeStruct((t, heads * dh), BF16),
        compiler_params=_params("parallel", "parallel"),
        name="stick_attention",
    )(q, kv, kv)


def _proj_residual_kernel(o_ref, w_ref, g_ref, h_ref, out_ref):
    a = _dot(o_ref[...], w_ref[...])
    out_ref[...] = h_ref[...] + _rms(a, g_ref[...])


def proj_residual(o, w, g, li, h, *, tm=512):
    t, d = h.shape
    kdim = o.shape[1]
    return pl.pallas_call(
        _proj_residual_kernel,
        grid=(t // tm,),
        in_specs=[
            pl.BlockSpec((tm, kdim), lambda i: (i, 0)),
            pl.BlockSpec((None, kdim, d), lambda i: (li, 0, 0)),
            _gain_spec(d, li),
            pl.BlockSpec((tm, d), lambda i: (i, 0)),
        ],
        out_specs=pl.BlockSpec((tm, d), lambda i: (i, 0)),
        out_shape=jax.ShapeDtypeStruct((t, d), F32),
        compiler_params=_params("parallel"),
        name="proj_residual",
    )(o, w, g.reshape(-1, 1, d), h)


def _mlp_kernel(h_ref, gpre_ref, wup_ref, wdn_ref, gpost_ref, out_ref, xn_ref, acc_ref):
    f = pl.program_id(1)

    @pl.when(f == 0)
    def _():
        xn_ref[...] = _rms(h_ref[...], gpre_ref[...]).astype(BF16)
        acc_ref[...] = jnp.zeros(acc_ref.shape, F32)

    u = jnp.maximum(_dot(xn_ref[...], wup_ref[...]), 0.0)
    acc_ref[...] += _dot((u * u).astype(BF16), wdn_ref[...])

    @pl.when(f == pl.num_programs(1) - 1)
    def _():
        out_ref[...] = h_ref[...] + _rms(acc_ref[...], gpost_ref[...])


def mlp(h, g_pre, w_up, w_down, g_post, li, *, tm=512, tf=1024):
    t, d = h.shape
    ff = w_up.shape[2]
    return pl.pallas_call(
        _mlp_kernel,
        grid=(t // tm, ff // tf),
        in_specs=[
            pl.BlockSpec((tm, d), lambda i, f: (i, 0)),
            _gain_spec(d, li),
            pl.BlockSpec((None, d, tf), lambda i, f: (li, 0, f)),
            pl.BlockSpec((None, tf, d), lambda i, f: (li, f, 0)),
            _gain_spec(d, li),
        ],
        out_specs=pl.BlockSpec((tm, d), lambda i, f: (i, 0)),
        out_shape=jax.ShapeDtypeStruct((t, d), F32),
        scratch_shapes=[pltpu.VMEM((tm, d), BF16), pltpu.VMEM((tm, d), F32)],
        compiler_params=_params("parallel", "arbitrary"),
        name="mlp",
    )(h, g_pre.reshape(-1, 1, d), w_up, w_down, g_post.reshape(-1, 1, d))


def _rope_tables(seq):
    pos = jnp.arange(seq, dtype=F32)
    inv_freq = ROPE_THETA ** (-jnp.arange(0, ROPE_DIM, 2, dtype=F32) / ROPE_DIM)
    ang = pos[:, None] * inv_freq[None, :]
    cos, sin = jnp.cos(ang), jnp.sin(ang)
    zero = jnp.zeros_like(cos)
    zero2 = jnp.zeros((seq, LANES - ROPE_DIM), F32)
    c = jnp.concatenate([cos, cos, zero2], axis=1)
    sa = jnp.concatenate([-sin, zero, zero2], axis=1)
    sb = jnp.concatenate([zero, sin, zero2], axis=1)
    return c, sa, sb


def _prep_mla_weights(w_dq_dkv, w_uq, w_ukv):
    nl = w_dq_dkv.shape[0]
    wd = jnp.pad(w_dq_dkv, ((0, 0), (0, 0), (0, LAT_PAD - w_dq_dkv.shape[2]))).astype(BF16)
    wq = w_uq.reshape(nl, Q_LORA, A_HEADS, NOPE_DIM + ROPE_DIM)
    wq = jnp.pad(wq, ((0, 0), (0, 0), (0, 0), (0, QK_PAD - NOPE_DIM - ROPE_DIM)))
    wq = wq.reshape(nl, Q_LORA, A_HEADS * QK_PAD).astype(BF16)
    wkv = w_ukv.reshape(nl, KV_LORA, A_HEADS, NOPE_DIM + V_DIM)
    wkv = jnp.concatenate(
        [wkv[..., :NOPE_DIM].reshape(nl, KV_LORA, -1), wkv[..., NOPE_DIM:].reshape(nl, KV_LORA, -1)],
        axis=2).astype(BF16)
    return wd, wq, wkv


def kernel(x, a_pre_g, a_post_g, a_w_dq_dkv, a_q_norm_g, a_kv_norm_g, a_w_uq, a_w_ukv, a_w_o,
           kv_norm_g, w_kv_shared, b_pre_g, b_post_g, b_w_q, b_w_o,
           mlp_pre_g, mlp_post_g, mlp_w_up, mlp_w_down):
    bsz, seq, d = x.shape
    n_a = a_pre_g.shape[0]
    n_b = b_pre_g.shape[0]
    h = x.reshape(bsz * seq, d)
    rope_c, rope_sa, rope_sb = _rope_tables(seq)
    wd, wq, wkv = _prep_mla_weights(a_w_dq_dkv, a_w_uq, a_w_ukv)
    a_wo, b_wq, b_wo = a_w_o.astype(BF16), b_w_q.astype(BF16), b_w_o.astype(BF16)
    w_up, w_down = mlp_w_up.astype(BF16), mlp_w_down.astype(BF16)
    w_kv = w_kv_shared.astype(BF16)[None]
    kv_sh = None
    for layer in range(n_a + n_b):
        if layer < n_a:
            i = layer
            q, k, v = mla_front(h, a_pre_g, wd, a_q_norm_g, a_kv_norm_g, wq, wkv, i,
                                rope_c, rope_sa, rope_sb, seq=seq)
            o = softmax_attention(q, k, v, batch=bsz, seq=seq, heads=A_HEADS, dk=QK_PAD, dv=V_DIM)
            h = proj_residual(o, a_wo, a_post_g, i, h)
        else:
            j = layer - n_a
            q = norm_matmul(h, b_pre_g, b_wq, j, scale=LOG2E * B_HEAD_DIM ** -0.5)
            o = stick_attention(q, kv_sh, batch=bsz, seq=seq, heads=B_HEADS, dh=B_HEAD_DIM)
            h = proj_residual(o, b_wo, b_post_g, j, h)
        h = mlp(h, mlp_pre_g, w_up, w_down, mlp_post_g, layer)
        if layer == n_a - 1:
            kv_sh = norm_matmul(h, kv_norm_g[None], w_kv, 0)
    return h.reshape(bsz, seq, d)
```

```python
import functools

import jax
import jax.numpy as jnp
from jax import lax
from jax.experimental import pallas as pl
from jax.experimental.pallas import tpu as pltpu

D_MODEL = 2048
A_HEADS = 16
Q_LORA = 512
KV_LORA = 512
NOPE_DIM = 128
ROPE_DIM = 64
V_DIM = 128
ROPE_THETA = 10000.0
B_HEADS = 16
B_HEAD_DIM = 128
NORM_EPS = 1e-6
LOG2E = 1.4426950408889634

LANES = 128
QK_PAD = 256
LAT_PAD = Q_LORA + KV_LORA + LANES
VMEM_LIMIT = 56 * 1024 * 1024

BF16 = jnp.bfloat16
F32 = jnp.float32


def _params(*sem):
    return pltpu.CompilerParams(dimension_semantics=sem, vmem_limit_bytes=VMEM_LIMIT)


def _rms(x, g):
    return x * lax.rsqrt(jnp.mean(x * x, axis=-1, keepdims=True) + NORM_EPS) * g


def _dot(a, b):
    return jnp.dot(a, b, preferred_element_type=F32)


def _dot_nt(a, b):
    return lax.dot_general(a, b, (((1,), (1,)), ((), ())), preferred_element_type=F32)


def _gain_spec(n, li):
    return pl.BlockSpec((None, 1, n), lambda *_: (li, 0, 0))


def _side_cast(w_all, layer, steps, step_of):
    _, r, c = w_all.shape
    rows = r // steps
    return (pl.BlockSpec((None, rows, c), lambda *ids: (layer, step_of(*ids), 0)),
            pl.BlockSpec((None, rows, c), lambda *ids: (0, step_of(*ids), 0)),
            jax.ShapeDtypeStruct((1, r, c), BF16))


def _rope(x, c, sa, sb):
    return x * c + pltpu.roll(x, 96, 1) * sa + pltpu.roll(x, 32, 1) * sb


def _norm_matmul_kernel(x_ref, g_ref, w_ref, o_ref, *, scale):
    y = _dot(_rms(x_ref[...], g_ref[...]).astype(BF16), w_ref[...])
    if scale != 1.0:
        y = y * scale
    o_ref[...] = y.astype(o_ref.dtype)


def norm_matmul(x, g, w, li, *, scale=1.0, tm=512, tn=2048):
    t, d = x.shape
    n = w.shape[2]
    return pl.pallas_call(
        functools.partial(_norm_matmul_kernel, scale=scale),
        grid=(n // tn, t // tm),
        in_specs=[
            pl.BlockSpec((tm, d), lambda j, i: (i, 0)),
            _gain_spec(d, li),
            pl.BlockSpec((None, d, tn), lambda j, i: (li, 0, j)),
        ],
        out_specs=pl.BlockSpec((tm, tn), lambda j, i: (i, j)),
        out_shape=jax.ShapeDtypeStruct((t, n), BF16),
        compiler_params=_params("parallel", "parallel"),
        name="norm_matmul",
    )(x, g.reshape(-1, 1, d), w)


def _mla_front_kernel(h_ref, gpre_ref, wd_ref, gq_ref, gkv_ref, wq_ref, wkv_ref,
                      c_ref, sa_ref, sb_ref, q_ref, k_ref, v_ref, *, scale):
    xn = _rms(h_ref[...], gpre_ref[...]).astype(BF16)
    lat = _dot(xn, wd_ref[...])
    cq = _rms(lat[:, :Q_LORA], gq_ref[...]).astype(BF16)
    ckv = _rms(lat[:, Q_LORA:Q_LORA + KV_LORA], gkv_ref[...]).astype(BF16)
    c, sa, sb = c_ref[...], sa_ref[...], sb_ref[...]
    k_rope = _rope(lat[:, Q_LORA + KV_LORA:], c, sa, sb).astype(BF16)
    hv = A_HEADS * NOPE_DIM
    for hh in range(A_HEADS):
        qh = _dot(cq, wq_ref[:, hh * QK_PAD:(hh + 1) * QK_PAD])
        q_ref[:, hh * QK_PAD:hh * QK_PAD + LANES] = (qh[:, :LANES] * scale).astype(BF16)
        q_ref[:, hh * QK_PAD + LANES:(hh + 1) * QK_PAD] = (
            _rope(qh[:, LANES:], c, sa, sb) * scale).astype(BF16)
        k_ref[:, hh * QK_PAD + LANES:(hh + 1) * QK_PAD] = k_rope
    for hh in range(0, A_HEADS, 2):
        kh = _dot(ckv, wkv_ref[:, hh * NOPE_DIM:(hh + 2) * NOPE_DIM]).astype(BF16)
        k_ref[:, hh * QK_PAD:hh * QK_PAD + LANES] = kh[:, :LANES]
        k_ref[:, (hh + 1) * QK_PAD:(hh + 1) * QK_PAD + LANES] = kh[:, LANES:]
        vh = _dot(ckv, wkv_ref[:, hv + hh * V_DIM:hv + (hh + 2) * V_DIM])
        v_ref[:, hh * V_DIM:(hh + 2) * V_DIM] = vh.astype(BF16)


def mla_front(h, g_pre, wd, g_q, g_kv, wq, wkv, li, rope_c, rope_sa, rope_sb, *, seq, tm=512):
    t, d = h.shape
    nqk = A_HEADS * QK_PAD
    nv = A_HEADS * V_DIM
    pos_blocks = seq // tm
    layer = lambda i: (li, 0, 0)
    return pl.pallas_call(
        functools.partial(_mla_front_kernel, scale=LOG2E * (NOPE_DIM + ROPE_DIM) ** -0.5),
        grid=(t // tm,),
        in_specs=[
            pl.BlockSpec((tm, d), lambda i: (i, 0)),
            _gain_spec(d, li),
            pl.BlockSpec((None, d, LAT_PAD), layer, pipeline_mode=pl.Buffered(1)),
            _gain_spec(Q_LORA, li),
            _gain_spec(KV_LORA, li),
            pl.BlockSpec((None, Q_LORA, nqk), layer, pipeline_mode=pl.Buffered(1)),
            pl.BlockSpec((None, KV_LORA, 2 * nv), layer, pipeline_mode=pl.Buffered(1)),
            pl.BlockSpec((tm, LANES), lambda i: (i % pos_blocks, 0)),
            pl.BlockSpec((tm, LANES), lambda i: (i % pos_blocks, 0)),
            pl.BlockSpec((tm, LANES), lambda i: (i % pos_blocks, 0)),
        ],
        out_specs=[
            pl.BlockSpec((tm, nqk), lambda i: (i, 0)),
            pl.BlockSpec((tm, nqk), lambda i: (i, 0)),
            pl.BlockSpec((tm, nv), lambda i: (i, 0)),
        ],
        out_shape=[
            jax.ShapeDtypeStruct((t, nqk), BF16),
            jax.ShapeDtypeStruct((t, nqk), BF16),
            jax.ShapeDtypeStruct((t, nv), BF16),
        ],
        compiler_params=_params("parallel"),
        name="mla_front",
    )(h, g_pre.reshape(-1, 1, d), wd, g_q.reshape(-1, 1, Q_LORA), g_kv.reshape(-1, 1, KV_LORA),
      wq, wkv, rope_c, rope_sa, rope_sb)


def _softmax_attn_kernel(q_ref, k_ref, v_ref, *rest, tq, hp, dk, dv):
    n_cast = len(rest) // 2
    o_ref = rest[n_cast]
    for w_ref, wo_ref in zip(rest[:n_cast], rest[n_cast + 1:]):
        wo_ref[...] = w_ref[...].astype(BF16)
    seq = q_ref.shape[0]
    row = lax.broadcasted_iota(jnp.int32, (tq, tq), 0)
    col = lax.broadcasted_iota(jnp.int32, (tq, tq), 1)
    causal = col <= row

    def scores(hh, i):
        lo, hi = i * tq, (i + 1) * tq
        kc = slice(hh * dk, (hh + 1) * dk)
        q = q_ref[lo:hi, kc]
        s_d = jnp.where(causal, _dot_nt(q, k_ref[lo:hi, kc]), -jnp.inf)
        m = jnp.max(s_d, axis=1, keepdims=True)
        s_o = None
        if i > 0:
            s_o = _dot_nt(q, k_ref[0:lo, kc])
            m = jnp.maximum(m, jnp.max(s_o, axis=1, keepdims=True))
        return s_d, s_o, m

    def weights_times_values(hh, i, s_d, s_o, m):
        lo, hi = i * tq, (i + 1) * tq
        vc = slice(hh * dv, (hh + 1) * dv)
        p_d = jnp.exp2(s_d - m)
        l = jnp.sum(p_d, axis=1, keepdims=True)
        o = _dot(p_d.astype(BF16), v_ref[lo:hi, vc])
        if i > 0:
            p_o = jnp.exp2(s_o - m)
            l = l + jnp.sum(p_o, axis=1, keepdims=True)
            o = o + _dot(p_o.astype(BF16), v_ref[0:lo, vc])
        o_ref[lo:hi, vc] = (o / l).astype(BF16)

    tiles = [(hh, i) for hh in range(hp) for i in range(seq // tq)]
    pending = scores(*tiles[0])
    for n, tile in enumerate(tiles):
        nxt = scores(*tiles[n + 1]) if n + 1 < len(tiles) else None
        weights_times_values(*tile, *pending)
        pending = nxt


def softmax_attention(q, k, v, *, batch, seq, heads, dk, dv, cast_f32=(), tq=256, hp=2):
    t = batch * seq
    ng = heads // hp
    in_specs = [
        pl.BlockSpec((seq, hp * dk), lambda b, g: (b, g)),
        pl.BlockSpec((seq, hp * dk), lambda b, g: (b, g)),
        pl.BlockSpec((seq, hp * dv), lambda b, g: (b, g)),
    ]
    out_specs = [pl.BlockSpec((seq, hp * dv), lambda b, g: (b, g))]
    out_shape = [jax.ShapeDtypeStruct((t, heads * dv), BF16)]
    args = [q, k, v]
    for w_all, layer in cast_f32:
        i_spec, o_spec, o_shape = _side_cast(w_all, layer, batch * ng, lambda b, g: b * ng + g)
        in_specs.append(i_spec)
        out_specs.append(o_spec)
        out_shape.append(o_shape)
        args.append(w_all)
    return pl.pallas_call(
        functools.partial(_softmax_attn_kernel, tq=tq, hp=hp, dk=dk, dv=dv),
        grid=(batch, ng),
        in_specs=in_specs,
        out_specs=out_specs,
        out_shape=out_shape,
        compiler_params=_params("parallel", "parallel"),
        name="softmax_attention",
    )(*args)


def _softplus2(t):
    return jnp.maximum(t, 0.0) + jnp.log(1.0 + jnp.exp2(-jnp.abs(t))) * LOG2E


def _stick_attn_kernel(q_ref, k_ref, v_ref, o_ref, *, tq, hp, dh):
    seq = q_ref.shape[0]
    reps = tq // LANES
    row = lax.broadcasted_iota(jnp.int32, (tq, tq), 0)
    col = lax.broadcasted_iota(jnp.int32, (tq, tq), 1)
    strict = col < row
    above = jnp.where(row > col, 1.0, 0.0).astype(BF16)

    def scores(hh, i):
        lo, hi = i * tq, (i + 1) * tq
        hc = slice(hh * dh, (hh + 1) * dh)
        z = _dot_nt(q_ref[lo:hi, hc], k_ref[0:hi, hc])
        parts = [None] * (i + 1)
        passed = None
        for b in range(i, -1, -1):
            zb = z[:, b * tq:(b + 1) * tq]
            sp = _softplus2(zb)
            if b == i:
                sp = jnp.where(strict, sp, 0.0)
            sp_bf = sp.astype(BF16)
            excl = _dot(sp_bf, above)
            parts[b] = (zb - sp, excl, passed)
            total = jnp.broadcast_to(excl[:, 0:1] + sp_bf[:, 0:1].astype(F32), (tq, LANES))
            passed = total if passed is None else passed + total
        return parts

    def weights_times_values(hh, i, parts):
        lo, hi = i * tq, (i + 1) * tq
        hc = slice(hh * dh, (hh + 1) * dh)
        a_blocks = []
        for b, (log_beta, excl, passed) in enumerate(parts):
            log_a = log_beta - excl
            if passed is not None:
                log_a = log_a - jnp.tile(passed, (1, reps))
            a = jnp.exp2(log_a)
            if b == i:
                a = jnp.where(strict, a, 0.0)
            a_blocks.append(a.astype(BF16))
        a_all = a_blocks[0] if i == 0 else jnp.concatenate(a_blocks, axis=1)
        o_ref[lo:hi, hc] = _dot(a_all, v_ref[0:hi, hc]).astype(BF16)

    tiles = [(hh, i) for hh in range(hp) for i in range(seq // tq)]
    pending = scores(*tiles[0])
    for n, tile in enumerate(tiles):
        nxt = scores(*tiles[n + 1]) if n + 1 < len(tiles) else None
        weights_times_values(*tile, pending)
        pending = nxt


def stick_attention(q, kv, *, batch, seq, heads, dh, tq=256, hp=2):
    t = batch * seq
    v_off = heads // hp
    return pl.pallas_call(
        functools.partial(_stick_attn_kernel, tq=tq, hp=hp, dh=dh),
        grid=(batch, heads // hp),
        in_specs=[
            pl.BlockSpec((seq, hp * dh), lambda b, g: (b, g)),
            pl.BlockSpec((seq, hp * dh), lambda b, g: (b, g)),
            pl.BlockSpec((seq, hp * dh), lambda b, g: (b, v_off + g)),
        ],
        out_specs=pl.BlockSpec((seq, hp * dh), lambda b, g: (b, g)),
        out_shape=jax.ShapeDtypeStruct((t, heads * dh), BF16),
        compiler_params=_params("parallel", "parallel"),
        name="stick_attention",
    )(q, kv, kv)


def _proj_residual_kernel(o_ref, w_ref, g_ref, h_ref, out_ref):
    a = _dot(o_ref[...], w_ref[...])
    out_ref[...] = h_ref[...] + _rms(a, g_ref[...])


def proj_residual(o, w, g, li, h, *, tm=512):
    t, d = h.shape
    kdim = o.shape[1]
    return pl.pallas_call(
        _proj_residual_kernel,
        grid=(t // tm,),
        in_specs=[
            pl.BlockSpec((tm, kdim), lambda i: (i, 0)),
            pl.BlockSpec((None, kdim, d), lambda i: (li, 0, 0)),
            _gain_spec(d, li),
            pl.BlockSpec((tm, d), lambda i: (i, 0)),
        ],
        out_specs=pl.BlockSpec((tm, d), lambda i: (i, 0)),
        out_shape=jax.ShapeDtypeStruct((t, d), F32),
        compiler_params=_params("parallel"),
        name="proj_residual",
    )(o, w, g.reshape(-1, 1, d), h)


def _mlp_kernel(h_ref, gpre_ref, wup_ref, wdn_ref, gpost_ref, *rest, cast_next):
    if cast_next:
        nup_ref, ndn_ref, out_ref, oup_ref, odn_ref, xn_ref, acc_ref = rest
        oup_ref[...] = nup_ref[...].astype(BF16)
        odn_ref[...] = ndn_ref[...].astype(BF16)
    else:
        out_ref, xn_ref, acc_ref = rest
    f = pl.program_id(1)

    @pl.when(f == 0)
    def _():
        xn_ref[...] = _rms(h_ref[...], gpre_ref[...]).astype(BF16)
        acc_ref[...] = jnp.zeros(acc_ref.shape, F32)

    u = jnp.maximum(_dot(xn_ref[...], wup_ref[...]), 0.0)
    acc_ref[...] += _dot((u * u).astype(BF16), wdn_ref[...])

    @pl.when(f == pl.num_programs(1) - 1)
    def _():
        out_ref[...] = h_ref[...] + _rms(acc_ref[...], gpost_ref[...])


def mlp(h, g_pre, w_up, w_down, g_post, li, wi, next_f32=None, *, tm=512, tf=1024):
    t, d = h.shape
    ff = w_up.shape[2]
    nf = ff // tf
    steps = (t // tm) * nf
    in_specs = [
        pl.BlockSpec((tm, d), lambda i, f: (i, 0)),
        _gain_spec(d, li),
        pl.BlockSpec((None, d, tf), lambda i, f: (wi, 0, f)),
        pl.BlockSpec((None, tf, d), lambda i, f: (wi, f, 0)),
        _gain_spec(d, li),
    ]
    out_specs = [pl.BlockSpec((tm, d), lambda i, f: (i, 0))]
    out_shape = [jax.ShapeDtypeStruct((t, d), F32)]
    args = [h, g_pre.reshape(-1, 1, d), w_up, w_down, g_post.reshape(-1, 1, d)]
    if next_f32 is not None:
        for w_all in next_f32[:2]:
            i_spec, o_spec, o_shape = _side_cast(w_all, next_f32[2], steps, lambda i, f: i * nf + f)
            in_specs.append(i_spec)
            out_specs.append(o_spec)
            out_shape.append(o_shape)
            args.append(w_all)
    return pl.pallas_call(
        functools.partial(_mlp_kernel, cast_next=next_f32 is not None),
        grid=(t // tm, nf),
        in_specs=in_specs,
        out_specs=out_specs,
        out_shape=out_shape,
        scratch_shapes=[pltpu.VMEM((tm, d), BF16), pltpu.VMEM((tm, d), F32)],
        compiler_params=_params("parallel", "arbitrary"),
        name="mlp",
    )(*args)


def _rope_tables(seq):
    pos = jnp.arange(seq, dtype=F32)
    inv_freq = ROPE_THETA ** (-jnp.arange(0, ROPE_DIM, 2, dtype=F32) / ROPE_DIM)
    ang = pos[:, None] * inv_freq[None, :]
    cos, sin = jnp.cos(ang), jnp.sin(ang)
    zero = jnp.zeros_like(cos)
    zero2 = jnp.zeros((seq, LANES - ROPE_DIM), F32)
    c = jnp.concatenate([cos, cos, zero2], axis=1)
    sa = jnp.concatenate([-sin, zero, zero2], axis=1)
    sb = jnp.concatenate([zero, sin, zero2], axis=1)
    return c, sa, sb


def _prep_mla_weights(w_dq_dkv, w_uq, w_ukv):
    nl = w_dq_dkv.shape[0]
    wd = jnp.pad(w_dq_dkv, ((0, 0), (0, 0), (0, LAT_PAD - w_dq_dkv.shape[2]))).astype(BF16)
    wq = w_uq.reshape(nl, Q_LORA, A_HEADS, NOPE_DIM + ROPE_DIM)
    wq = jnp.pad(wq, ((0, 0), (0, 0), (0, 0), (0, QK_PAD - NOPE_DIM - ROPE_DIM)))
    wq = wq.reshape(nl, Q_LORA, A_HEADS * QK_PAD).astype(BF16)
    wkv = w_ukv.reshape(nl, KV_LORA, A_HEADS, NOPE_DIM + V_DIM)
    wkv = jnp.concatenate(
        [wkv[..., :NOPE_DIM].reshape(nl, KV_LORA, -1), wkv[..., NOPE_DIM:].reshape(nl, KV_LORA, -1)],
        axis=2).astype(BF16)
    return wd, wq, wkv


def kernel(x, a_pre_g, a_post_g, a_w_dq_dkv, a_q_norm_g, a_kv_norm_g, a_w_uq, a_w_ukv, a_w_o,
           kv_norm_g, w_kv_shared, b_pre_g, b_post_g, b_w_q, b_w_o,
           mlp_pre_g, mlp_post_g, mlp_w_up, mlp_w_down):
    bsz, seq, d = x.shape
    n_a = a_pre_g.shape[0]
    n_b = b_pre_g.shape[0]
    assert n_a >= 1, "the MLP weight cast rides on the first MLA layer's attention call"
    h = x.reshape(bsz * seq, d)
    rope_c, rope_sa, rope_sb = _rope_tables(seq)
    wd, wq, wkv = _prep_mla_weights(a_w_dq_dkv, a_w_uq, a_w_ukv)
    a_wo = a_w_o.astype(BF16)
    w_up = w_down = b_wq = b_wo = w_kv = None
    kv_sh = None
    for layer in range(n_a + n_b):
        if layer < n_a:
            i = layer
            q, k, v = mla_front(h, a_pre_g, wd, a_q_norm_g, a_kv_norm_g, wq, wkv, i,
                                rope_c, rope_sa, rope_sb, seq=seq)
            if layer == 0:
                o, w_up, w_down, b_wq, b_wo, w_kv = softmax_attention(
                    q, k, v, batch=bsz, seq=seq, heads=A_HEADS, dk=QK_PAD, dv=V_DIM,
                    cast_f32=[(mlp_w_up, 0), (mlp_w_down, 0), (b_w_q.reshape(1, n_b * d, -1), 0),
                              (b_w_o.reshape(1, -1, d), 0), (w_kv_shared[None], 0)])
                b_wq = b_wq.reshape(n_b, d, -1)
                b_wo = b_wo.reshape(n_b, -1, d)
            else:
                o, = softmax_attention(q, k, v, batch=bsz, seq=seq, heads=A_HEADS, dk=QK_PAD, dv=V_DIM)
            h = proj_residual(o, a_wo, a_post_g, i, h)
        else:
            j = layer - n_a
            q = norm_matmul(h, b_pre_g, b_wq, j, scale=LOG2E * B_HEAD_DIM ** -0.5)
            o = stick_attention(q, kv_sh, batch=bsz, seq=seq, heads=B_HEADS, dh=B_HEAD_DIM)
            h = proj_residual(o, b_wo, b_post_g, j, h)
        if layer + 1 < n_a + n_b:
            h, w_up, w_down = mlp(h, mlp_pre_g, w_up, w_down, mlp_post_g, layer, 0,
                                  (mlp_w_up, mlp_w_down, layer + 1))
        else:
            h, = mlp(h, mlp_pre_g, w_up, w_down, mlp_post_g, layer, 0)
        if layer == n_a - 1:
            kv_sh = norm_matmul(h, kv_norm_g[None], w_kv, 0)
    return h.reshape(bsz, seq, d)
```

```python
import functools

import jax
import jax.numpy as jnp
from jax import lax
from jax.experimental import pallas as pl
from jax.experimental.pallas import tpu as pltpu

D_MODEL = 2048
A_HEADS = 16
Q_LORA = 512
KV_LORA = 512
NOPE_DIM = 128
ROPE_DIM = 64
V_DIM = 128
ROPE_THETA = 10000.0
B_HEADS = 16
B_HEAD_DIM = 128
NORM_EPS = 1e-6
LOG2E = 1.4426950408889634

LANES = 128
QK_PAD = 256
LAT_PAD = Q_LORA + KV_LORA + LANES
VMEM_LIMIT = 56 * 1024 * 1024

BF16 = jnp.bfloat16
F32 = jnp.float32


def _params(*sem):
    return pltpu.CompilerParams(dimension_semantics=sem, vmem_limit_bytes=VMEM_LIMIT)


def _rms(x, g):
    return x * lax.rsqrt(jnp.mean(x * x, axis=-1, keepdims=True) + NORM_EPS) * g


def _dot(a, b):
    return jnp.dot(a, b, preferred_element_type=F32)


def _dot_nt(a, b):
    return lax.dot_general(a, b, (((1,), (1,)), ((), ())), preferred_element_type=F32)


def _gain_spec(n, li):
    return pl.BlockSpec((None, 1, n), lambda *_: (li, 0, 0))


def _side_cast(w_all, layer, steps, step_of):
    _, r, c = w_all.shape
    rows = r // steps
    return (pl.BlockSpec((None, rows, c), lambda *ids: (layer, step_of(*ids), 0)),
            pl.BlockSpec((None, rows, c), lambda *ids: (0, step_of(*ids), 0)),
            jax.ShapeDtypeStruct((1, r, c), BF16))


def _rope(x, c, sa, sb):
    return x * c + pltpu.roll(x, 96, 1) * sa + pltpu.roll(x, 32, 1) * sb


def _norm_matmul_kernel(x_ref, g_ref, w_ref, o_ref, *, scale):
    y = _dot(_rms(x_ref[...], g_ref[...]).astype(BF16), w_ref[...])
    if scale != 1.0:
        y = y * scale
    o_ref[...] = y.astype(o_ref.dtype)


def norm_matmul(x, g, w, li, *, scale=1.0, tm=512, tn=2048):
    t, d = x.shape
    n = w.shape[2]
    return pl.pallas_call(
        functools.partial(_norm_matmul_kernel, scale=scale),
        grid=(n // tn, t // tm),
        in_specs=[
            pl.BlockSpec((tm, d), lambda j, i: (i, 0)),
            _gain_spec(d, li),
            pl.BlockSpec((None, d, tn), lambda j, i: (li, 0, j)),
        ],
        out_specs=pl.BlockSpec((tm, tn), lambda j, i: (i, j)),
        out_shape=jax.ShapeDtypeStruct((t, n), BF16),
        compiler_params=_params("parallel", "parallel"),
        name="norm_matmul",
    )(x, g.reshape(-1, 1, d), w)


def _mla_front_kernel(h_ref, gpre_ref, wd_ref, gq_ref, gkv_ref, wq_ref, wkv_ref,
                      c_ref, sa_ref, sb_ref, q_ref, k_ref, v_ref, *, scale):
    xn = _rms(h_ref[...], gpre_ref[...]).astype(BF16)
    lat = _dot(xn, wd_ref[...])
    cq = _rms(lat[:, :Q_LORA], gq_ref[...]).astype(BF16)
    ckv = _rms(lat[:, Q_LORA:Q_LORA + KV_LORA], gkv_ref[...]).astype(BF16)
    c, sa, sb = c_ref[...], sa_ref[...], sb_ref[...]
    k_rope = _rope(lat[:, Q_LORA + KV_LORA:], c, sa, sb).astype(BF16)
    hv = A_HEADS * NOPE_DIM
    for hh in range(A_HEADS):
        qh = _dot(cq, wq_ref[:, hh * QK_PAD:(hh + 1) * QK_PAD])
        q_ref[:, hh * QK_PAD:hh * QK_PAD + LANES] = (qh[:, :LANES] * scale).astype(BF16)
        q_ref[:, hh * QK_PAD + LANES:(hh + 1) * QK_PAD] = (
            _rope(qh[:, LANES:], c, sa, sb) * scale).astype(BF16)
        k_ref[:, hh * QK_PAD + LANES:(hh + 1) * QK_PAD] = k_rope
    for hh in range(0, A_HEADS, 2):
        kh = _dot(ckv, wkv_ref[:, hh * NOPE_DIM:(hh + 2) * NOPE_DIM]).astype(BF16)
        k_ref[:, hh * QK_PAD:hh * QK_PAD + LANES] = kh[:, :LANES]
        k_ref[:, (hh + 1) * QK_PAD:(hh + 1) * QK_PAD + LANES] = kh[:, LANES:]
        vh = _dot(ckv, wkv_ref[:, hv + hh * V_DIM:hv + (hh + 2) * V_DIM])
        v_ref[:, hh * V_DIM:(hh + 2) * V_DIM] = vh.astype(BF16)


def mla_front(h, g_pre, wd, g_q, g_kv, wq, wkv, li, rope_c, rope_sa, rope_sb, *, seq, tm=512):
    t, d = h.shape
    nqk = A_HEADS * QK_PAD
    nv = A_HEADS * V_DIM
    pos_blocks = seq // tm
    layer = lambda i: (li, 0, 0)
    return pl.pallas_call(
        functools.partial(_mla_front_kernel, scale=LOG2E * (NOPE_DIM + ROPE_DIM) ** -0.5),
        grid=(t // tm,),
        in_specs=[
            pl.BlockSpec((tm, d), lambda i: (i, 0)),
            _gain_spec(d, li),
            pl.BlockSpec((None, d, LAT_PAD), layer, pipeline_mode=pl.Buffered(1)),
            _gain_spec(Q_LORA, li),
            _gain_spec(KV_LORA, li),
            pl.BlockSpec((None, Q_LORA, nqk), layer, pipeline_mode=pl.Buffered(1)),
            pl.BlockSpec((None, KV_LORA, 2 * nv), layer, pipeline_mode=pl.Buffered(1)),
            pl.BlockSpec((tm, LANES), lambda i: (i % pos_blocks, 0)),
            pl.BlockSpec((tm, LANES), lambda i: (i % pos_blocks, 0)),
            pl.BlockSpec((tm, LANES), lambda i: (i % pos_blocks, 0)),
        ],
        out_specs=[
            pl.BlockSpec((tm, nqk), lambda i: (i, 0)),
            pl.BlockSpec((tm, nqk), lambda i: (i, 0)),
            pl.BlockSpec((tm, nv), lambda i: (i, 0)),
        ],
        out_shape=[
            jax.ShapeDtypeStruct((t, nqk), BF16),
            jax.ShapeDtypeStruct((t, nqk), BF16),
            jax.ShapeDtypeStruct((t, nv), BF16),
        ],
        compiler_params=_params("parallel"),
        name="mla_front",
    )(h, g_pre.reshape(-1, 1, d), wd, g_q.reshape(-1, 1, Q_LORA), g_kv.reshape(-1, 1, KV_LORA),
      wq, wkv, rope_c, rope_sa, rope_sb)


def _softmax_attn_kernel(q_ref, k_ref, v_ref, *rest, tq, hp, dk, dv):
    n_cast = len(rest) // 2
    o_ref = rest[n_cast]
    for w_ref, wo_ref in zip(rest[:n_cast], rest[n_cast + 1:]):
        wo_ref[...] = w_ref[...].astype(BF16)
    seq = q_ref.shape[0]
    row = lax.broadcasted_iota(jnp.int32, (tq, tq), 0)
    col = lax.broadcasted_iota(jnp.int32, (tq, tq), 1)
    causal = col <= row

    def scores(hh, i):
        lo, hi = i * tq, (i + 1) * tq
        kc = slice(hh * dk, (hh + 1) * dk)
        q = q_ref[lo:hi, kc]
        s_d = jnp.where(causal, _dot_nt(q, k_ref[lo:hi, kc]), -jnp.inf)
        m = jnp.max(s_d, axis=1, keepdims=True)
        s_o = None
        if i > 0:
            s_o = _dot_nt(q, k_ref[0:lo, kc])
            m = jnp.maximum(m, jnp.max(s_o, axis=1, keepdims=True))
        return s_d, s_o, m

    def weights_times_values(hh, i, s_d, s_o, m):
        lo, hi = i * tq, (i + 1) * tq
        vc = slice(hh * dv, (hh + 1) * dv)
        p_d = jnp.exp2(s_d - m)
        l = jnp.sum(p_d, axis=1, keepdims=True)
        o = _dot(p_d.astype(BF16), v_ref[lo:hi, vc])
        if i > 0:
            p_o = jnp.exp2(s_o - m)
            l = l + jnp.sum(p_o, axis=1, keepdims=True)
            o = o + _dot(p_o.astype(BF16), v_ref[0:lo, vc])
        o_ref[lo:hi, vc] = (o / l).astype(BF16)

    tiles = [(hh, i) for hh in range(hp) for i in range(seq // tq)]
    pending = scores(*tiles[0])
    for n, tile in enumerate(tiles):
        nxt = scores(*tiles[n + 1]) if n + 1 < len(tiles) else None
        weights_times_values(*tile, *pending)
        pending = nxt


def softmax_attention(q, k, v, *, batch, seq, heads, dk, dv, cast_f32=(), tq=256, hp=2):
    t = batch * seq
    ng = heads // hp
    in_specs = [
        pl.BlockSpec((seq, hp * dk), lambda b, g: (b, g)),
        pl.BlockSpec((seq, hp * dk), lambda b, g: (b, g)),
        pl.BlockSpec((seq, hp * dv), lambda b, g: (b, g)),
    ]
    out_specs = [pl.BlockSpec((seq, hp * dv), lambda b, g: (b, g))]
    out_shape = [jax.ShapeDtypeStruct((t, heads * dv), BF16)]
    args = [q, k, v]
    for w_all, layer in cast_f32:
        i_spec, o_spec, o_shape = _side_cast(w_all, layer, batch * ng, lambda b, g: b * ng + g)
        in_specs.append(i_spec)
        out_specs.append(o_spec)
        out_shape.append(o_shape)
        args.append(w_all)
    return pl.pallas_call(
        functools.partial(_softmax_attn_kernel, tq=tq, hp=hp, dk=dk, dv=dv),
        grid=(batch, ng),
        in_specs=in_specs,
        out_specs=out_specs,
        out_shape=out_shape,
        compiler_params=_params("parallel", "parallel"),
        name="softmax_attention",
    )(*args)


def _softplus2(t):
    return jnp.maximum(t, 0.0) + jnp.log(1.0 + jnp.exp2(-jnp.abs(t))) * LOG2E


def _stick_attn_kernel(q_ref, k_ref, v_ref, o_ref, *, tq, hp, dh):
    seq = q_ref.shape[0]
    reps = tq // LANES
    row = lax.broadcasted_iota(jnp.int32, (tq, tq), 0)
    col = lax.broadcasted_iota(jnp.int32, (tq, tq), 1)
    strict = col < row
    above = jnp.where(row > col, 1.0, 0.0).astype(BF16)

    def scores(hh, i):
        lo, hi = i * tq, (i + 1) * tq
        hc = slice(hh * dh, (hh + 1) * dh)
        z = _dot_nt(q_ref[lo:hi, hc], k_ref[0:hi, hc])
        parts = [None] * (i + 1)
        passed = None
        for b in range(i, -1, -1):
            zb = z[:, b * tq:(b + 1) * tq]
            sp = _softplus2(zb)
            if b == i:
                sp = jnp.where(strict, sp, 0.0)
            sp_bf = sp.astype(BF16)
            excl = _dot(sp_bf, above)
            parts[b] = (zb - sp, excl, passed)
            total = jnp.broadcast_to(excl[:, 0:1] + sp_bf[:, 0:1].astype(F32), (tq, LANES))
            passed = total if passed is None else passed + total
        return parts

    def weights_times_values(hh, i, parts):
        lo, hi = i * tq, (i + 1) * tq
        hc = slice(hh * dh, (hh + 1) * dh)
        a_blocks = []
        for b, (log_beta, excl, passed) in enumerate(parts):
            log_a = log_beta - excl
            if passed is not None:
                log_a = log_a - jnp.tile(passed, (1, reps))
            a = jnp.exp2(log_a)
            if b == i:
                a = jnp.where(strict, a, 0.0)
            a_blocks.append(a.astype(BF16))
        a_all = a_blocks[0] if i == 0 else jnp.concatenate(a_blocks, axis=1)
        o_ref[lo:hi, hc] = _dot(a_all, v_ref[0:hi, hc]).astype(BF16)

    tiles = [(hh, i) for hh in range(hp) for i in range(seq // tq)]
    pending = scores(*tiles[0])
    for n, tile in enumerate(tiles):
        nxt = scores(*tiles[n + 1]) if n + 1 < len(tiles) else None
        weights_times_values(*tile, pending)
        pending = nxt


def stick_attention(q, kv, *, batch, seq, heads, dh, tq=256, hp=2):
    t = batch * seq
    v_off = heads // hp
    return pl.pallas_call(
        functools.partial(_stick_attn_kernel, tq=tq, hp=hp, dh=dh),
        grid=(batch, heads // hp),
        in_specs=[
            pl.BlockSpec((seq, hp * dh), lambda b, g: (b, g)),
            pl.BlockSpec((seq, hp * dh), lambda b, g: (b, g)),
            pl.BlockSpec((seq, hp * dh), lambda b, g: (b, v_off + g)),
        ],
        out_specs=pl.BlockSpec((seq, hp * dh), lambda b, g: (b, g)),
        out_shape=jax.ShapeDtypeStruct((t, heads * dh), BF16),
        compiler_params=_params("parallel", "parallel"),
        name="stick_attention",
    )(q, kv, kv)


def _proj_residual_kernel(o_ref, w_ref, g_ref, h_ref, out_ref):
    a = _dot(o_ref[...], w_ref[...])
    out_ref[...] = h_ref[...] + _rms(a, g_ref[...])


def proj_residual(o, w, g, li, h, *, tm=512):
    t, d = h.shape
    kdim = o.shape[1]
    return pl.pallas_call(
        _proj_residual_kernel,
        grid=(t // tm,),
        in_specs=[
            pl.BlockSpec((tm, kdim), lambda i: (i, 0)),
            pl.BlockSpec((None, kdim, d), lambda i: (li, 0, 0)),
            _gain_spec(d, li),
            pl.BlockSpec((tm, d), lambda i: (i, 0)),
        ],
        out_specs=pl.BlockSpec((tm, d), lambda i: (i, 0)),
        out_shape=jax.ShapeDtypeStruct((t, d), F32),
        compiler_params=_params("parallel"),
        name="proj_residual",
    )(o, w, g.reshape(-1, 1, d), h)


def _mlp_kernel(h_ref, hnext_ref, gpre_ref, wup_ref, wdn_ref, gpost_ref, *rest, cast_next):
    if cast_next:
        nup_ref, ndn_ref, out_ref, oup_ref, odn_ref, xn_ref = rest
        oup_ref[...] = nup_ref[...].astype(BF16)
        odn_ref[...] = ndn_ref[...].astype(BF16)
    else:
        out_ref, xn_ref = rest
    i, f = pl.program_id(0), pl.program_id(1)
    last = pl.num_programs(1) - 1
    slot = i % 2

    @pl.when((i == 0) & (f == 0))
    def _():
        xn_ref[0] = _rms(h_ref[...], gpre_ref[...]).astype(BF16)

    def chunk():
        u = jnp.maximum(_dot(xn_ref[slot], wup_ref[...]), 0.0)
        return _dot((u * u).astype(BF16), wdn_ref[...])

    @pl.when(f == 0)
    def _():
        out_ref[...] = chunk()

    @pl.when((f > 0) & (f < last))
    def _():
        out_ref[...] += chunk()

    @pl.when(f == last)
    def _():
        xn_ref[1 - slot] = _rms(hnext_ref[...], gpre_ref[...]).astype(BF16)
        out_ref[...] = h_ref[...] + _rms(out_ref[...] + chunk(), gpost_ref[...])


def mlp(h, g_pre, w_up, w_down, g_post, li, wi, next_f32=None, *, tm=512, tf=1024):
    t, d = h.shape
    ff = w_up.shape[2]
    nt, nf = t // tm, ff // tf
    assert nf >= 2
    steps = nt * nf
    in_specs = [
        pl.BlockSpec((tm, d), lambda i, f: (i, 0)),
        pl.BlockSpec((tm, d), lambda i, f: (jnp.minimum(i + 1, nt - 1), 0)),
        _gain_spec(d, li),
        pl.BlockSpec((None, d, tf), lambda i, f: (wi, 0, f)),
        pl.BlockSpec((None, tf, d), lambda i, f: (wi, f, 0)),
        _gain_spec(d, li),
    ]
    out_specs = [pl.BlockSpec((tm, d), lambda i, f: (i, 0))]
    out_shape = [jax.ShapeDtypeStruct((t, d), F32)]
    args = [h, h, g_pre.reshape(-1, 1, d), w_up, w_down, g_post.reshape(-1, 1, d)]
    if next_f32 is not None:
        for w_all in next_f32[:2]:
            i_spec, o_spec, o_shape = _side_cast(w_all, next_f32[2], steps, lambda i, f: i * nf + f)
            in_specs.append(i_spec)
            out_specs.append(o_spec)
            out_shape.append(o_shape)
            args.append(w_all)
    return pl.pallas_call(
        functools.partial(_mlp_kernel, cast_next=next_f32 is not None),
        grid=(nt, nf),
        in_specs=in_specs,
        out_specs=out_specs,
        out_shape=out_shape,
        scratch_shapes=[pltpu.VMEM((2, tm, d), BF16)],
        compiler_params=_params("arbitrary", "arbitrary"),
        name="mlp",
    )(*args)


def _rope_tables(seq):
    pos = jnp.arange(seq, dtype=F32)
    inv_freq = ROPE_THETA ** (-jnp.arange(0, ROPE_DIM, 2, dtype=F32) / ROPE_DIM)
    ang = pos[:, None] * inv_freq[None, :]
    cos, sin = jnp.cos(ang), jnp.sin(ang)
    zero = jnp.zeros_like(cos)
    zero2 = jnp.zeros((seq, LANES - ROPE_DIM), F32)
    c = jnp.concatenate([cos, cos, zero2], axis=1)
    sa = jnp.concatenate([-sin, zero, zero2], axis=1)
    sb = jnp.concatenate([zero, sin, zero2], axis=1)
    return c, sa, sb


def _prep_mla_weights(w_dq_dkv, w_uq, w_ukv):
    nl = w_dq_dkv.shape[0]
    wd = jnp.pad(w_dq_dkv, ((0, 0), (0, 0), (0, LAT_PAD - w_dq_dkv.shape[2]))).astype(BF16)
    wq = w_uq.reshape(nl, Q_LORA, A_HEADS, NOPE_DIM + ROPE_DIM)
    wq = jnp.pad(wq, ((0, 0), (0, 0), (0, 0), (0, QK_PAD - NOPE_DIM - ROPE_DIM)))
    wq = wq.reshape(nl, Q_LORA, A_HEADS * QK_PAD).astype(BF16)
    wkv = w_ukv.reshape(nl, KV_LORA, A_HEADS, NOPE_DIM + V_DIM)
    wkv = jnp.concatenate(
        [wkv[..., :NOPE_DIM].reshape(nl, KV_LORA, -1), wkv[..., NOPE_DIM:].reshape(nl, KV_LORA, -1)],
        axis=2).astype(BF16)
    return wd, wq, wkv


def kernel(x, a_pre_g, a_post_g, a_w_dq_dkv, a_q_norm_g, a_kv_norm_g, a_w_uq, a_w_ukv, a_w_o,
           kv_norm_g, w_kv_shared, b_pre_g, b_post_g, b_w_q, b_w_o,
           mlp_pre_g, mlp_post_g, mlp_w_up, mlp_w_down):
    bsz, seq, d = x.shape
    n_a = a_pre_g.shape[0]
    n_b = b_pre_g.shape[0]
    assert n_a >= 1, "the MLP weight cast rides on the first MLA layer's attention call"
    h = x.reshape(bsz * seq, d)
    rope_c, rope_sa, rope_sb = _rope_tables(seq)
    wd, wq, wkv = _prep_mla_weights(a_w_dq_dkv, a_w_uq, a_w_ukv)
    w_up = w_down = a_wo = b_wq = b_wo = w_kv = None
    kv_sh = None
    for layer in range(n_a + n_b):
        if layer < n_a:
            i = layer
            q, k, v = mla_front(h, a_pre_g, wd, a_q_norm_g, a_kv_norm_g, wq, wkv, i,
                                rope_c, rope_sa, rope_sb, seq=seq)
            if layer == 0:
                o, w_up, w_down, a_wo, b_wq, b_wo, w_kv = softmax_attention(
                    q, k, v, batch=bsz, seq=seq, heads=A_HEADS, dk=QK_PAD, dv=V_DIM,
                    cast_f32=[(mlp_w_up, 0), (mlp_w_down, 0), (a_w_o.reshape(1, -1, d), 0),
                              (b_w_q.reshape(1, n_b * d, -1), 0), (b_w_o.reshape(1, -1, d), 0),
                              (w_kv_shared[None], 0)])
                a_wo = a_wo.reshape(n_a, -1, d)
                b_wq = b_wq.reshape(n_b, d, -1)
                b_wo = b_wo.reshape(n_b, -1, d)
            else:
                o, = softmax_attention(q, k, v, batch=bsz, seq=seq, heads=A_HEADS, dk=QK_PAD, dv=V_DIM)
            h = proj_residual(o, a_wo, a_post_g, i, h)
        else:
            j = layer - n_a
            q = norm_matmul(h, b_pre_g, b_wq, j, scale=LOG2E * B_HEAD_DIM ** -0.5)
            o = stick_attention(q, kv_sh, batch=bsz, seq=seq, heads=B_HEADS, dh=B_HEAD_DIM)
            h = proj_residual(o, b_wo, b_post_g, j, h)
        if layer + 1 < n_a + n_b:
            h, w_up, w_down = mlp(h, mlp_pre_g, w_up, w_down, mlp_post_g, layer, 0,
                                  (mlp_w_up, mlp_w_down, layer + 1))
        else:
            h, = mlp(h, mlp_pre_g, w_up, w_down, mlp_post_g, layer, 0)
        if layer == n_a - 1:
            kv_sh = norm_matmul(h, kv_norm_g[None], w_kv, 0)
    return h.reshape(bsz, seq, d)
```

```python
import functools

import jax
import jax.numpy as jnp
from jax import lax
from jax.experimental import pallas as pl
from jax.experimental.pallas import tpu as pltpu

D_MODEL = 2048
A_HEADS = 16
Q_LORA = 512
KV_LORA = 512
NOPE_DIM = 128
ROPE_DIM = 64
V_DIM = 128
ROPE_THETA = 10000.0
B_HEADS = 16
B_HEAD_DIM = 128
NORM_EPS = 1e-6
LOG2E = 1.4426950408889634

LANES = 128
QK_PAD = 256
LAT_PAD = Q_LORA + KV_LORA + LANES
VMEM_LIMIT = 56 * 1024 * 1024

BF16 = jnp.bfloat16
F32 = jnp.float32


def _params(*sem):
    return pltpu.CompilerParams(dimension_semantics=sem, vmem_limit_bytes=VMEM_LIMIT)


def _rms(x, g):
    return x * lax.rsqrt(jnp.mean(x * x, axis=-1, keepdims=True) + NORM_EPS) * g


def _dot(a, b):
    return jnp.dot(a, b, preferred_element_type=F32)


def _dot_nt(a, b):
    return lax.dot_general(a, b, (((1,), (1,)), ((), ())), preferred_element_type=F32)


def _gain_spec(n, li):
    return pl.BlockSpec((None, 1, n), lambda *_: (li, 0, 0))


def _side_cast(w_all, layer, steps, step_of):
    _, r, c = w_all.shape
    rows = r // steps
    return (pl.BlockSpec((None, rows, c), lambda *ids: (layer, step_of(*ids), 0)),
            pl.BlockSpec((None, rows, c), lambda *ids: (0, step_of(*ids), 0)),
            jax.ShapeDtypeStruct((1, r, c), BF16))


def _rope(x, c, sa, sb):
    return x * c + pltpu.roll(x, 96, 1) * sa + pltpu.roll(x, 32, 1) * sb


def _norm_matmul_kernel(x_ref, g_ref, w_ref, o_ref, *, scale):
    y = _dot(_rms(x_ref[...], g_ref[...]).astype(BF16), w_ref[...])
    if scale != 1.0:
        y = y * scale
    o_ref[...] = y.astype(o_ref.dtype)


def norm_matmul(x, g, w, li, *, scale=1.0, tm=512, tn=2048):
    t, d = x.shape
    n = w.shape[2]
    return pl.pallas_call(
        functools.partial(_norm_matmul_kernel, scale=scale),
        grid=(n // tn, t // tm),
        in_specs=[
            pl.BlockSpec((tm, d), lambda j, i: (i, 0)),
            _gain_spec(d, li),
            pl.BlockSpec((None, d, tn), lambda j, i: (li, 0, j)),
        ],
        out_specs=pl.BlockSpec((tm, tn), lambda j, i: (i, j)),
        out_shape=jax.ShapeDtypeStruct((t, n), BF16),
        compiler_params=_params("parallel", "parallel"),
        name="norm_matmul",
    )(x, g.reshape(-1, 1, d), w)


def _mla_front_kernel(h_ref, gpre_ref, wd_ref, gq_ref, gkv_ref, wq_ref, wkv_ref,
                      c_ref, sa_ref, sb_ref, q_ref, k_ref, v_ref, *, scale):
    xn = _rms(h_ref[...], gpre_ref[...]).astype(BF16)
    lat = _dot(xn, wd_ref[...])
    cq = _rms(lat[:, :Q_LORA], gq_ref[...]).astype(BF16)
    ckv = _rms(lat[:, Q_LORA:Q_LORA + KV_LORA], gkv_ref[...]).astype(BF16)
    c, sa, sb = c_ref[...], sa_ref[...], sb_ref[...]
    k_rope = _rope(lat[:, Q_LORA + KV_LORA:], c, sa, sb).astype(BF16)
    hv = A_HEADS * NOPE_DIM
    for hh in range(A_HEADS):
        qh = _dot(cq, wq_ref[:, hh * QK_PAD:(hh + 1) * QK_PAD])
        q_ref[:, hh * QK_PAD:hh * QK_PAD + LANES] = (qh[:, :LANES] * scale).astype(BF16)
        q_ref[:, hh * QK_PAD + LANES:(hh + 1) * QK_PAD] = (
            _rope(qh[:, LANES:], c, sa, sb) * scale).astype(BF16)
        k_ref[:, hh * QK_PAD + LANES:(hh + 1) * QK_PAD] = k_rope
    for hh in range(0, A_HEADS, 2):
        kh = _dot(ckv, wkv_ref[:, hh * NOPE_DIM:(hh + 2) * NOPE_DIM]).astype(BF16)
        k_ref[:, hh * QK_PAD:hh * QK_PAD + LANES] = kh[:, :LANES]
        k_ref[:, (hh + 1) * QK_PAD:(hh + 1) * QK_PAD + LANES] = kh[:, LANES:]
        vh = _dot(ckv, wkv_ref[:, hv + hh * V_DIM:hv + (hh + 2) * V_DIM])
        v_ref[:, hh * V_DIM:(hh + 2) * V_DIM] = vh.astype(BF16)


def mla_front(h, g_pre, wd, g_q, g_kv, wq, wkv, li, rope_c, rope_sa, rope_sb, *, seq, tm=512):
    t, d = h.shape
    nqk = A_HEADS * QK_PAD
    nv = A_HEADS * V_DIM
    pos_blocks = seq // tm
    layer = lambda i: (li, 0, 0)
    return pl.pallas_call(
        functools.partial(_mla_front_kernel, scale=LOG2E * (NOPE_DIM + ROPE_DIM) ** -0.5),
        grid=(t // tm,),
        in_specs=[
            pl.BlockSpec((tm, d), lambda i: (i, 0)),
            _gain_spec(d, li),
            pl.BlockSpec((None, d, LAT_PAD), layer, pipeline_mode=pl.Buffered(1)),
            _gain_spec(Q_LORA, li),
            _gain_spec(KV_LORA, li),
            pl.BlockSpec((None, Q_LORA, nqk), layer, pipeline_mode=pl.Buffered(1)),
            pl.BlockSpec((None, KV_LORA, 2 * nv), layer, pipeline_mode=pl.Buffered(1)),
            pl.BlockSpec((tm, LANES), lambda i: (i % pos_blocks, 0)),
            pl.BlockSpec((tm, LANES), lambda i: (i % pos_blocks, 0)),
            pl.BlockSpec((tm, LANES), lambda i: (i % pos_blocks, 0)),
        ],
        out_specs=[
            pl.BlockSpec((tm, nqk), lambda i: (i, 0)),
            pl.BlockSpec((tm, nqk), lambda i: (i, 0)),
            pl.BlockSpec((tm, nv), lambda i: (i, 0)),
        ],
        out_shape=[
            jax.ShapeDtypeStruct((t, nqk), BF16),
            jax.ShapeDtypeStruct((t, nqk), BF16),
            jax.ShapeDtypeStruct((t, nv), BF16),
        ],
        compiler_params=_params("parallel"),
        name="mla_front",
    )(h, g_pre.reshape(-1, 1, d), wd, g_q.reshape(-1, 1, Q_LORA), g_kv.reshape(-1, 1, KV_LORA),
      wq, wkv, rope_c, rope_sa, rope_sb)


def _softmax_attn_kernel(q_ref, k_ref, v_ref, *rest, tq, hp, dk, dv):
    n_cast = len(rest) // 2
    o_ref = rest[n_cast]
    for w_ref, wo_ref in zip(rest[:n_cast], rest[n_cast + 1:]):
        wo_ref[...] = w_ref[...].astype(BF16)
    seq = q_ref.shape[0]
    row = lax.broadcasted_iota(jnp.int32, (tq, tq), 0)
    col = lax.broadcasted_iota(jnp.int32, (tq, tq), 1)
    causal = col <= row

    def scores(hh, i):
        lo, hi = i * tq, (i + 1) * tq
        kc = slice(hh * dk, (hh + 1) * dk)
        q = q_ref[lo:hi, kc]
        s_d = jnp.where(causal, _dot_nt(q, k_ref[lo:hi, kc]), -jnp.inf)
        m = jnp.max(s_d, axis=1, keepdims=True)
        s_o = None
        if i > 0:
            s_o = _dot_nt(q, k_ref[0:lo, kc])
            m = jnp.maximum(m, jnp.max(s_o, axis=1, keepdims=True))
        return s_d, s_o, m

    def weights_times_values(hh, i, s_d, s_o, m):
        lo, hi = i * tq, (i + 1) * tq
        vc = slice(hh * dv, (hh + 1) * dv)
        p_d = jnp.exp2(s_d - m)
        l = jnp.sum(p_d, axis=1, keepdims=True)
        o = _dot(p_d.astype(BF16), v_ref[lo:hi, vc])
        if i > 0:
            p_o = jnp.exp2(s_o - m)
            l = l + jnp.sum(p_o, axis=1, keepdims=True)
            o = o + _dot(p_o.astype(BF16), v_ref[0:lo, vc])
        o_ref[lo:hi, vc] = (o / l).astype(BF16)

    tiles = [(hh, i) for hh in range(hp) for i in range(seq // tq)]
    pending = scores(*tiles[0])
    for n, tile in enumerate(tiles):
        nxt = scores(*tiles[n + 1]) if n + 1 < len(tiles) else None
        weights_times_values(*tile, *pending)
        pending = nxt


def softmax_attention(q, k, v, *, batch, seq, heads, dk, dv, cast_f32=(), tq=256, hp=2):
    t = batch * seq
    ng = heads // hp
    in_specs = [
        pl.BlockSpec((seq, hp * dk), lambda b, g: (b, g)),
        pl.BlockSpec((seq, hp * dk), lambda b, g: (b, g)),
        pl.BlockSpec((seq, hp * dv), lambda b, g: (b, g)),
    ]
    out_specs = [pl.BlockSpec((seq, hp * dv), lambda b, g: (b, g))]
    out_shape = [jax.ShapeDtypeStruct((t, heads * dv), BF16)]
    args = [q, k, v]
    for w_all, layer in cast_f32:
        i_spec, o_spec, o_shape = _side_cast(w_all, layer, batch * ng, lambda b, g: b * ng + g)
        in_specs.append(i_spec)
        out_specs.append(o_spec)
        out_shape.append(o_shape)
        args.append(w_all)
    return pl.pallas_call(
        functools.partial(_softmax_attn_kernel, tq=tq, hp=hp, dk=dk, dv=dv),
        grid=(batch, ng),
        in_specs=in_specs,
        out_specs=out_specs,
        out_shape=out_shape,
        compiler_params=_params("parallel", "parallel"),
        name="softmax_attention",
    )(*args)


def _softplus2(t):
    return jnp.maximum(t, 0.0) + jnp.log(1.0 + jnp.exp2(-jnp.abs(t))) * LOG2E


def _stick_attn_kernel(q_ref, k_ref, v_ref, o_ref, *, tq, hp, dh):
    seq = q_ref.shape[0]
    reps = tq // LANES
    row = lax.broadcasted_iota(jnp.int32, (tq, tq), 0)
    col = lax.broadcasted_iota(jnp.int32, (tq, tq), 1)
    strict = col < row
    above = jnp.where(row > col, 1.0, 0.0).astype(BF16)

    def scores(hh, i):
        lo, hi = i * tq, (i + 1) * tq
        hc = slice(hh * dh, (hh + 1) * dh)
        z = _dot_nt(q_ref[lo:hi, hc], k_ref[0:hi, hc])
        parts = [None] * (i + 1)
        passed = None
        for b in range(i, -1, -1):
            zb = z[:, b * tq:(b + 1) * tq]
            sp = _softplus2(zb)
            if b == i:
                sp = jnp.where(strict, sp, 0.0)
            sp_bf = sp.astype(BF16)
            excl = _dot(sp_bf, above)
            parts[b] = (zb - sp, excl, passed)
            total = jnp.broadcast_to(excl[:, 0:1] + sp_bf[:, 0:1].astype(F32), (tq, LANES))
            passed = total if passed is None else passed + total
        return parts

    def weights_times_values(hh, i, parts):
        lo, hi = i * tq, (i + 1) * tq
        hc = slice(hh * dh, (hh + 1) * dh)
        a_blocks = []
        for b, (log_beta, excl, passed) in enumerate(parts):
            log_a = log_beta - excl
            if passed is not None:
                log_a = log_a - jnp.tile(passed, (1, reps))
            a = jnp.exp2(log_a)
            if b == i:
                a = jnp.where(strict, a, 0.0)
            a_blocks.append(a.astype(BF16))
        a_all = a_blocks[0] if i == 0 else jnp.concatenate(a_blocks, axis=1)
        o_ref[lo:hi, hc] = _dot(a_all, v_ref[0:hi, hc]).astype(BF16)

    tiles = [(hh, i) for hh in range(hp) for i in range(seq // tq)]
    pending = scores(*tiles[0])
    for n, tile in enumerate(tiles):
        nxt = scores(*tiles[n + 1]) if n + 1 < len(tiles) else None
        weights_times_values(*tile, pending)
        pending = nxt


def stick_attention(q, kv, *, batch, seq, heads, dh, tq=256, hp=2):
    t = batch * seq
    v_off = heads // hp
    return pl.pallas_call(
        functools.partial(_stick_attn_kernel, tq=tq, hp=hp, dh=dh),
        grid=(batch, heads // hp),
        in_specs=[
            pl.BlockSpec((seq, hp * dh), lambda b, g: (b, g)),
            pl.BlockSpec((seq, hp * dh), lambda b, g: (b, g)),
            pl.BlockSpec((seq, hp * dh), lambda b, g: (b, v_off + g)),
        ],
        out_specs=pl.BlockSpec((seq, hp * dh), lambda b, g: (b, g)),
        out_shape=jax.ShapeDtypeStruct((t, heads * dh), BF16),
        compiler_params=_params("parallel", "parallel"),
        name="stick_attention",
    )(q, kv, kv)


def _proj_residual_kernel(o_ref, w_ref, g_ref, h_ref, out_ref):
    a = _dot(o_ref[...], w_ref[...])
    out_ref[...] = h_ref[...] + _rms(a, g_ref[...])


def proj_residual(o, w, g, li, h, *, tm=512):
    t, d = h.shape
    kdim = o.shape[1]
    return pl.pallas_call(
        _proj_residual_kernel,
        grid=(t // tm,),
        in_specs=[
            pl.BlockSpec((tm, kdim), lambda i: (i, 0)),
            pl.BlockSpec((None, kdim, d), lambda i: (li, 0, 0)),
            _gain_spec(d, li),
            pl.BlockSpec((tm, d), lambda i: (i, 0)),
        ],
        out_specs=pl.BlockSpec((tm, d), lambda i: (i, 0)),
        out_shape=jax.ShapeDtypeStruct((t, d), F32),
        compiler_params=_params("parallel"),
        name="proj_residual",
    )(o, w, g.reshape(-1, 1, d), h)


def _mlp_kernel(h_ref, gpre_ref, wup_ref, wdn_ref, gpost_ref, *rest, cast_next):
    if cast_next:
        nup_ref, ndn_ref, out_ref, oup_ref, odn_ref, xn_ref = rest
        oup_ref[...] = nup_ref[...].astype(BF16)
        odn_ref[...] = ndn_ref[...].astype(BF16)
    else:
        out_ref, xn_ref = rest
    f = pl.program_id(1)
    last = pl.num_programs(1) - 1

    def chunk():
        u = jnp.maximum(_dot(xn_ref[...], wup_ref[...]), 0.0)
        return _dot((u * u).astype(BF16), wdn_ref[...])

    @pl.when(f == 0)
    def _():
        xn_ref[...] = _rms(h_ref[...], gpre_ref[...]).astype(BF16)
        out_ref[...] = chunk()

    @pl.when((f > 0) & (f < last))
    def _():
        out_ref[...] += chunk()

    @pl.when(f == last)
    def _():
        out_ref[...] = h_ref[...] + _rms(out_ref[...] + chunk(), gpost_ref[...])


def mlp(h, g_pre, w_up, w_down, g_post, li, wi, next_f32=None, *, tm=1024, tf=512):
    t, d = h.shape
    ff = w_up.shape[2]
    nt, nf = t // tm, ff // tf
    assert nf >= 2
    steps = nt * nf
    in_specs = [
        pl.BlockSpec((tm, d), lambda i, f: (i, 0)),
        _gain_spec(d, li),
        pl.BlockSpec((None, d, tf), lambda i, f: (wi, 0, f)),
        pl.BlockSpec((None, tf, d), lambda i, f: (wi, f, 0)),
        _gain_spec(d, li),
    ]
    out_specs = [pl.BlockSpec((tm, d), lambda i, f: (i, 0))]
    out_shape = [jax.ShapeDtypeStruct((t, d), F32)]
    args = [h, g_pre.reshape(-1, 1, d), w_up, w_down, g_post.reshape(-1, 1, d)]
    if next_f32 is not None:
        for w_all in next_f32[:2]:
            i_spec, o_spec, o_shape = _side_cast(w_all, next_f32[2], steps, lambda i, f: i * nf + f)
            in_specs.append(i_spec)
            out_specs.append(o_spec)
            out_shape.append(o_shape)
            args.append(w_all)
    return pl.pallas_call(
        functools.partial(_mlp_kernel, cast_next=next_f32 is not None),
        grid=(nt, nf),
        in_specs=in_specs,
        out_specs=out_specs,
        out_shape=out_shape,
        scratch_shapes=[pltpu.VMEM((tm, d), BF16)],
        compiler_params=_params("parallel", "arbitrary"),
        name="mlp",
    )(*args)


def _rope_tables(seq):
    pos = jnp.arange(seq, dtype=F32)
    inv_freq = ROPE_THETA ** (-jnp.arange(0, ROPE_DIM, 2, dtype=F32) / ROPE_DIM)
    ang = pos[:, None] * inv_freq[None, :]
    cos, sin = jnp.cos(ang), jnp.sin(ang)
    zero = jnp.zeros_like(cos)
    zero2 = jnp.zeros((seq, LANES - ROPE_DIM), F32)
    c = jnp.concatenate([cos, cos, zero2], axis=1)
    sa = jnp.concatenate([-sin, zero, zero2], axis=1)
    sb = jnp.concatenate([zero, sin, zero2], axis=1)
    return c, sa, sb


def _prep_mla_weights(w_dq_dkv, w_uq, w_ukv):
    nl = w_dq_dkv.shape[0]
    wd = jnp.pad(w_dq_dkv, ((0, 0), (0, 0), (0, LAT_PAD - w_dq_dkv.shape[2]))).astype(BF16)
    wq = w_uq.reshape(nl, Q_LORA, A_HEADS, NOPE_DIM + ROPE_DIM)
    wq = jnp.pad(wq, ((0, 0), (0, 0), (0, 0), (0, QK_PAD - NOPE_DIM - ROPE_DIM)))
    wq = wq.reshape(nl, Q_LORA, A_HEADS * QK_PAD).astype(BF16)
    wkv = w_ukv.reshape(nl, KV_LORA, A_HEADS, NOPE_DIM + V_DIM)
    wkv = jnp.concatenate(
        [wkv[..., :NOPE_DIM].reshape(nl, KV_LORA, -1), wkv[..., NOPE_DIM:].reshape(nl, KV_LORA, -1)],
        axis=2).astype(BF16)
    return wd, wq, wkv


def kernel(x, a_pre_g, a_post_g, a_w_dq_dkv, a_q_norm_g, a_kv_norm_g, a_w_uq, a_w_ukv, a_w_o,
           kv_norm_g, w_kv_shared, b_pre_g, b_post_g, b_w_q, b_w_o,
           mlp_pre_g, mlp_post_g, mlp_w_up, mlp_w_down):
    bsz, seq, d = x.shape
    n_a = a_pre_g.shape[0]
    n_b = b_pre_g.shape[0]
    assert n_a >= 1, "the MLP weight cast rides on the first MLA layer's attention call"
    h = x.reshape(bsz * seq, d)
    rope_c, rope_sa, rope_sb = _rope_tables(seq)
    wd, wq, wkv = _prep_mla_weights(a_w_dq_dkv, a_w_uq, a_w_ukv)
    w_up = w_down = a_wo = b_wq = b_wo = w_kv = None
    kv_sh = None
    for layer in range(n_a + n_b):
        if layer < n_a:
            i = layer
            q, k, v = mla_front(h, a_pre_g, wd, a_q_norm_g, a_kv_norm_g, wq, wkv, i,
                                rope_c, rope_sa, rope_sb, seq=seq)
            if layer == 0:
                o, w_up, w_down, a_wo, b_wq, b_wo, w_kv = softmax_attention(
                    q, k, v, batch=bsz, seq=seq, heads=A_HEADS, dk=QK_PAD, dv=V_DIM,
                    cast_f32=[(mlp_w_up, 0), (mlp_w_down, 0), (a_w_o.reshape(1, -1, d), 0),
                              (b_w_q.reshape(1, n_b * d, -1), 0), (b_w_o.reshape(1, -1, d), 0),
                              (w_kv_shared[None], 0)])
                a_wo = a_wo.reshape(n_a, -1, d)
                b_wq = b_wq.reshape(n_b, d, -1)
                b_wo = b_wo.reshape(n_b, -1, d)
            else:
                o, = softmax_attention(q, k, v, batch=bsz, seq=seq, heads=A_HEADS, dk=QK_PAD, dv=V_DIM)
            h = proj_residual(o, a_wo, a_post_g, i, h)
        else:
            j = layer - n_a
            q = norm_matmul(h, b_pre_g, b_wq, j, scale=LOG2E * B_HEAD_DIM ** -0.5)
            o = stick_attention(q, kv_sh, batch=bsz, seq=seq, heads=B_HEADS, dh=B_HEAD_DIM)
            h = proj_residual(o, b_wo, b_post_g, j, h)
        if layer + 1 < n_a + n_b:
            h, w_up, w_down = mlp(h, mlp_pre_g, w_up, w_down, mlp_post_g, layer, 0,
                                  (mlp_w_up, mlp_w_down, layer + 1))
        else:
            h, = mlp(h, mlp_pre_g, w_up, w_down, mlp_post_g, layer, 0)
        if layer == n_a - 1:
            kv_sh = norm_matmul(h, kv_norm_g[None], w_kv, 0)
    return h.reshape(bsz, seq, d)
```

```python
import functools

import jax
import jax.numpy as jnp
from jax import lax
from jax.experimental import pallas as pl
from jax.experimental.pallas import tpu as pltpu

D_MODEL = 2048
A_HEADS = 16
Q_LORA = 512
KV_LORA = 512
NOPE_DIM = 128
ROPE_DIM = 64
V_DIM = 128
ROPE_THETA = 10000.0
B_HEADS = 16
B_HEAD_DIM = 128
NORM_EPS = 1e-6
LOG2E = 1.4426950408889634

LANES = 128
QK_PAD = 256
VMEM_LIMIT = 56 * 1024 * 1024

BF16 = jnp.bfloat16
F32 = jnp.float32


def _params(*sem):
    return pltpu.CompilerParams(dimension_semantics=sem, vmem_limit_bytes=VMEM_LIMIT)


def _rms(x, g):
    return x * lax.rsqrt(jnp.mean(x * x, axis=-1, keepdims=True) + NORM_EPS) * g


def _dot(a, b):
    return jnp.dot(a, b, preferred_element_type=F32)


def _dot_nt(a, b):
    return lax.dot_general(a, b, (((1,), (1,)), ((), ())), preferred_element_type=F32)


def _gain_spec(n, li):
    return pl.BlockSpec((None, 1, n), lambda *_: (li, 0, 0))


def _side_cast(w_all, layer, steps, step_of):
    _, r, c = w_all.shape
    rows = r // steps
    return (pl.BlockSpec((None, rows, c), lambda *ids: (layer, step_of(*ids), 0)),
            pl.BlockSpec((None, rows, c), lambda *ids: (0, step_of(*ids), 0)),
            jax.ShapeDtypeStruct((1, r, c), BF16))


def _rope(x, c, sa, sb):
    return x * c + pltpu.roll(x, 96, 1) * sa + pltpu.roll(x, 32, 1) * sb


def _norm_matmul_kernel(x_ref, g_ref, w_ref, o_ref, *, scale):
    y = _dot(_rms(x_ref[...], g_ref[...]).astype(BF16), w_ref[...])
    if scale != 1.0:
        y = y * scale
    o_ref[...] = y.astype(o_ref.dtype)


def norm_matmul(x, g, w, li, *, scale=1.0, tm=512, tn=2048):
    t, d = x.shape
    n = w.shape[2]
    return pl.pallas_call(
        functools.partial(_norm_matmul_kernel, scale=scale),
        grid=(n // tn, t // tm),
        in_specs=[
            pl.BlockSpec((tm, d), lambda j, i: (i, 0)),
            _gain_spec(d, li),
            pl.BlockSpec((None, d, tn), lambda j, i: (li, 0, j)),
        ],
        out_specs=pl.BlockSpec((tm, tn), lambda j, i: (i, j)),
        out_shape=jax.ShapeDtypeStruct((t, n), BF16),
        compiler_params=_params("parallel", "parallel"),
        name="norm_matmul",
    )(x, g.reshape(-1, 1, d), w)


def _mla_front_kernel(h_ref, gpre_ref, wd_ref, gq_ref, gkv_ref, wq_ref, wkv_ref,
                      c_ref, sa_ref, sb_ref, q_ref, k_ref, v_ref, *, scale):
    tm = h_ref.shape[0]
    dqk = NOPE_DIM + ROPE_DIM
    xn = _rms(h_ref[...], gpre_ref[...]).astype(BF16)
    lat = _dot(xn, wd_ref[...])
    cq = _rms(lat[:, :Q_LORA], gq_ref[...]).astype(BF16)
    ckv = _rms(lat[:, Q_LORA:Q_LORA + KV_LORA], gkv_ref[...]).astype(BF16)
    c, sa, sb = c_ref[...], sa_ref[...], sb_ref[...]
    zeros = jnp.zeros((tm, LANES - ROPE_DIM), F32)
    k_rope = _rope(jnp.concatenate([lat[:, Q_LORA + KV_LORA:], zeros], axis=1), c, sa, sb).astype(BF16)
    low = lax.broadcasted_iota(jnp.int32, (tm, LANES), 1) < ROPE_DIM

    def put_q(hh, nope, rope):
        q_ref[:, hh * QK_PAD:hh * QK_PAD + LANES] = (nope * scale).astype(BF16)
        q_ref[:, hh * QK_PAD + LANES:(hh + 1) * QK_PAD] = (_rope(rope, c, sa, sb) * scale).astype(BF16)

    for hh in range(0, A_HEADS, 2):
        q2 = _dot(cq, wq_ref[:, hh * dqk:(hh + 2) * dqk])
        mid, top = q2[:, LANES:2 * LANES], q2[:, 2 * LANES:]
        mid_r, top_r = pltpu.roll(mid, ROPE_DIM, 1), pltpu.roll(top, ROPE_DIM, 1)
        put_q(hh, q2[:, :LANES], jnp.where(low, mid, 0.0))
        put_q(hh + 1, jnp.where(low, mid_r, top_r), jnp.where(low, top_r, 0.0))
    for hh in range(A_HEADS):
        kv = _dot(ckv, wkv_ref[:, hh * 2 * LANES:(hh + 1) * 2 * LANES]).astype(BF16)
        k_ref[:, hh * QK_PAD:hh * QK_PAD + LANES] = kv[:, :LANES]
        k_ref[:, hh * QK_PAD + LANES:(hh + 1) * QK_PAD] = k_rope
        v_ref[:, hh * V_DIM:(hh + 1) * V_DIM] = kv[:, LANES:]


def mla_front(h, g_pre, wd, g_q, g_kv, wq, wkv, li, rope_c, rope_sa, rope_sb, *, seq, tm=512):
    t, d = h.shape
    nqk = A_HEADS * QK_PAD
    nv = A_HEADS * V_DIM
    pos_blocks = seq // tm
    layer = lambda i: (li, 0, 0)
    return pl.pallas_call(
        functools.partial(_mla_front_kernel, scale=LOG2E * (NOPE_DIM + ROPE_DIM) ** -0.5),
        grid=(t // tm,),
        in_specs=[
            pl.BlockSpec((tm, d), lambda i: (i, 0)),
            _gain_spec(d, li),
            pl.BlockSpec((None, d, wd.shape[2]), layer, pipeline_mode=pl.Buffered(1)),
            _gain_spec(Q_LORA, li),
            _gain_spec(KV_LORA, li),
            pl.BlockSpec((None, Q_LORA, wq.shape[2]), layer, pipeline_mode=pl.Buffered(1)),
            pl.BlockSpec((None, KV_LORA, 2 * nv), layer, pipeline_mode=pl.Buffered(1)),
            pl.BlockSpec((tm, LANES), lambda i: (i % pos_blocks, 0)),
            pl.BlockSpec((tm, LANES), lambda i: (i % pos_blocks, 0)),
            pl.BlockSpec((tm, LANES), lambda i: (i % pos_blocks, 0)),
        ],
        out_specs=[
            pl.BlockSpec((tm, nqk), lambda i: (i, 0)),
            pl.BlockSpec((tm, nqk), lambda i: (i, 0)),
            pl.BlockSpec((tm, nv), lambda i: (i, 0)),
        ],
        out_shape=[
            jax.ShapeDtypeStruct((t, nqk), BF16),
            jax.ShapeDtypeStruct((t, nqk), BF16),
            jax.ShapeDtypeStruct((t, nv), BF16),
        ],
        compiler_params=_params("parallel"),
        name="mla_front",
    )(h, g_pre.reshape(-1, 1, d), wd, g_q.reshape(-1, 1, Q_LORA), g_kv.reshape(-1, 1, KV_LORA),
      wq, wkv, rope_c, rope_sa, rope_sb)


def _softmax_attn_kernel(q_ref, k_ref, v_ref, *rest, tq, hp, dk, dv):
    n_cast = len(rest) // 2
    o_ref = rest[n_cast]
    for w_ref, wo_ref in zip(rest[:n_cast], rest[n_cast + 1:]):
        wo_ref[...] = w_ref[...].astype(BF16)
    seq = q_ref.shape[0]
    row = lax.broadcasted_iota(jnp.int32, (tq, tq), 0)
    col = lax.broadcasted_iota(jnp.int32, (tq, tq), 1)
    causal = col <= row

    def scores(hh, i):
        lo, hi = i * tq, (i + 1) * tq
        kc = slice(hh * dk, (hh + 1) * dk)
        q = q_ref[lo:hi, kc]
        s_d = jnp.where(causal, _dot_nt(q, k_ref[lo:hi, kc]), -jnp.inf)
        m = jnp.max(s_d, axis=1, keepdims=True)
        s_o = None
        if i > 0:
            s_o = _dot_nt(q, k_ref[0:lo, kc])
            m = jnp.maximum(m, jnp.max(s_o, axis=1, keepdims=True))
        return s_d, s_o, m

    def weights_times_values(hh, i, s_d, s_o, m):
        lo, hi = i * tq, (i + 1) * tq
        vc = slice(hh * dv, (hh + 1) * dv)
        p_d = jnp.exp2(s_d - m)
        l = jnp.sum(p_d, axis=1, keepdims=True)
        o = _dot(p_d.astype(BF16), v_ref[lo:hi, vc])
        if i > 0:
            p_o = jnp.exp2(s_o - m)
            l = l + jnp.sum(p_o, axis=1, keepdims=True)
            o = o + _dot(p_o.astype(BF16), v_ref[0:lo, vc])
        o_ref[lo:hi, vc] = (o / l).astype(BF16)

    tiles = [(hh, i) for hh in range(hp) for i in range(seq // tq)]
    pending = scores(*tiles[0])
    for n, tile in enumerate(tiles):
        nxt = scores(*tiles[n + 1]) if n + 1 < len(tiles) else None
        weights_times_values(*tile, *pending)
        pending = nxt


def softmax_attention(q, k, v, *, batch, seq, heads, dk, dv, cast_f32=(), tq=256, hp=2):
    t = batch * seq
    ng = heads // hp
    in_specs = [
        pl.BlockSpec((seq, hp * dk), lambda b, g: (b, g)),
        pl.BlockSpec((seq, hp * dk), lambda b, g: (b, g)),
        pl.BlockSpec((seq, hp * dv), lambda b, g: (b, g)),
    ]
    out_specs = [pl.BlockSpec((seq, hp * dv), lambda b, g: (b, g))]
    out_shape = [jax.ShapeDtypeStruct((t, heads * dv), BF16)]
    args = [q, k, v]
    for w_all, layer in cast_f32:
        i_spec, o_spec, o_shape = _side_cast(w_all, layer, batch * ng, lambda b, g: b * ng + g)
        in_specs.append(i_spec)
        out_specs.append(o_spec)
        out_shape.append(o_shape)
        args.append(w_all)
    return pl.pallas_call(
        functools.partial(_softmax_attn_kernel, tq=tq, hp=hp, dk=dk, dv=dv),
        grid=(batch, ng),
        in_specs=in_specs,
        out_specs=out_specs,
        out_shape=out_shape,
        compiler_params=_params("parallel", "parallel"),
        name="softmax_attention",
    )(*args)


def _softplus2(t):
    return jnp.maximum(t, 0.0) + jnp.log(1.0 + jnp.exp2(-jnp.abs(t))) * LOG2E


def _stick_attn_kernel(q_ref, k_ref, v_ref, o_ref, *, tq, hp, dh):
    seq = q_ref.shape[0]
    reps = tq // LANES
    row = lax.broadcasted_iota(jnp.int32, (tq, tq), 0)
    col = lax.broadcasted_iota(jnp.int32, (tq, tq), 1)
    strict = col < row
    above = jnp.where(row > col, 1.0, 0.0).astype(BF16)

    def scores(hh, i):
        lo, hi = i * tq, (i + 1) * tq
        hc = slice(hh * dh, (hh + 1) * dh)
        z = _dot_nt(q_ref[lo:hi, hc], k_ref[0:hi, hc])
        parts = [None] * (i + 1)
        passed = None
        for b in range(i, -1, -1):
            zb = z[:, b * tq:(b + 1) * tq]
            sp = _softplus2(zb)
            if b == i:
                sp = jnp.where(strict, sp, 0.0)
            sp_bf = sp.astype(BF16)
            excl = _dot(sp_bf, above)
            parts[b] = (zb - sp, excl, passed)
            total = jnp.broadcast_to(excl[:, 0:1] + sp_bf[:, 0:1].astype(F32), (tq, LANES))
            passed = total if passed is None else passed + total
        return parts

    def weights_times_values(hh, i, parts):
        lo, hi = i * tq, (i + 1) * tq
        hc = slice(hh * dh, (hh + 1) * dh)
        a_blocks = []
        for b, (log_beta, excl, passed) in enumerate(parts):
            log_a = log_beta - excl
            if passed is not None:
                log_a = log_a - jnp.tile(passed, (1, reps))
            a = jnp.exp2(log_a)
            if b == i:
                a = jnp.where(strict, a, 0.0)
            a_blocks.append(a.astype(BF16))
        a_all = a_blocks[0] if i == 0 else jnp.concatenate(a_blocks, axis=1)
        o_ref[lo:hi, hc] = _dot(a_all, v_ref[0:hi, hc]).astype(BF16)

    tiles = [(hh, i) for hh in range(hp) for i in range(seq // tq)]
    pending = scores(*tiles[0])
    for n, tile in enumerate(tiles):
        nxt = scores(*tiles[n + 1]) if n + 1 < len(tiles) else None
        weights_times_values(*tile, pending)
        pending = nxt


def stick_attention(q, kv, *, batch, seq, heads, dh, tq=256, hp=2):
    t = batch * seq
    v_off = heads // hp
    return pl.pallas_call(
        functools.partial(_stick_attn_kernel, tq=tq, hp=hp, dh=dh),
        grid=(batch, heads // hp),
        in_specs=[
            pl.BlockSpec((seq, hp * dh), lambda b, g: (b, g)),
            pl.BlockSpec((seq, hp * dh), lambda b, g: (b, g)),
            pl.BlockSpec((seq, hp * dh), lambda b, g: (b, v_off + g)),
        ],
        out_specs=pl.BlockSpec((seq, hp * dh), lambda b, g: (b, g)),
        out_shape=jax.ShapeDtypeStruct((t, heads * dh), BF16),
        compiler_params=_params("parallel", "parallel"),
        name="stick_attention",
    )(q, kv, kv)


def _proj_residual_kernel(o_ref, w_ref, g_ref, h_ref, out_ref):
    a = _dot(o_ref[...], w_ref[...])
    out_ref[...] = h_ref[...] + _rms(a, g_ref[...])


def proj_residual(o, w, g, li, h, *, tm=512):
    t, d = h.shape
    kdim = o.shape[1]
    return pl.pallas_call(
        _proj_residual_kernel,
        grid=(t // tm,),
        in_specs=[
            pl.BlockSpec((tm, kdim), lambda i: (i, 0)),
            pl.BlockSpec((None, kdim, d), lambda i: (li, 0, 0)),
            _gain_spec(d, li),
            pl.BlockSpec((tm, d), lambda i: (i, 0)),
        ],
        out_specs=pl.BlockSpec((tm, d), lambda i: (i, 0)),
        out_shape=jax.ShapeDtypeStruct((t, d), F32),
        compiler_params=_params("parallel"),
        name="proj_residual",
    )(o, w, g.reshape(-1, 1, d), h)


def _mlp_kernel(h_ref, gpre_ref, wup_ref, wdn_ref, gpost_ref, *rest, cast_next):
    if cast_next:
        nup_ref, ndn_ref, out_ref, oup_ref, odn_ref, xn_ref = rest
        oup_ref[...] = nup_ref[...].astype(BF16)
        odn_ref[...] = ndn_ref[...].astype(BF16)
    else:
        out_ref, xn_ref = rest
    f = pl.program_id(1)
    last = pl.num_programs(1) - 1

    def chunk():
        u = jnp.maximum(_dot(xn_ref[...], wup_ref[...]), 0.0)
        return _dot((u * u).astype(BF16), wdn_ref[...])

    @pl.when(f == 0)
    def _():
        xn_ref[...] = _rms(h_ref[...], gpre_ref[...]).astype(BF16)
        out_ref[...] = chunk()

    @pl.when((f > 0) & (f < last))
    def _():
        out_ref[...] += chunk()

    @pl.when(f == last)
    def _():
        out_ref[...] = h_ref[...] + _rms(out_ref[...] + chunk(), gpost_ref[...])


def mlp(h, g_pre, w_up, w_down, g_post, li, wi, next_f32=None, *, tm=1024, tf=512):
    t, d = h.shape
    ff = w_up.shape[2]
    nt, nf = t // tm, ff // tf
    assert nf >= 2
    steps = nt * nf
    in_specs = [
        pl.BlockSpec((tm, d), lambda i, f: (i, 0)),
        _gain_spec(d, li),
        pl.BlockSpec((None, d, tf), lambda i, f: (wi, 0, f)),
        pl.BlockSpec((None, tf, d), lambda i, f: (wi, f, 0)),
        _gain_spec(d, li),
    ]
    out_specs = [pl.BlockSpec((tm, d), lambda i, f: (i, 0))]
    out_shape = [jax.ShapeDtypeStruct((t, d), F32)]
    args = [h, g_pre.reshape(-1, 1, d), w_up, w_down, g_post.reshape(-1, 1, d)]
    if next_f32 is not None:
        for w_all in next_f32[:2]:
            i_spec, o_spec, o_shape = _side_cast(w_all, next_f32[2], steps, lambda i, f: i * nf + f)
            in_specs.append(i_spec)
            out_specs.append(o_spec)
            out_shape.append(o_shape)
            args.append(w_all)
    return pl.pallas_call(
        functools.partial(_mlp_kernel, cast_next=next_f32 is not None),
        grid=(nt, nf),
        in_specs=in_specs,
        out_specs=out_specs,
        out_shape=out_shape,
        scratch_shapes=[pltpu.VMEM((tm, d), BF16)],
        compiler_params=_params("parallel", "arbitrary"),
        name="mlp",
    )(*args)


def _rope_tables(seq):
    pos = jnp.arange(seq, dtype=F32)
    inv_freq = ROPE_THETA ** (-jnp.arange(0, ROPE_DIM, 2, dtype=F32) / ROPE_DIM)
    ang = pos[:, None] * inv_freq[None, :]
    cos, sin = jnp.cos(ang), jnp.sin(ang)
    zero = jnp.zeros_like(cos)
    zero2 = jnp.zeros((seq, LANES - ROPE_DIM), F32)
    c = jnp.concatenate([cos, cos, zero2], axis=1)
    sa = jnp.concatenate([-sin, zero, zero2], axis=1)
    sb = jnp.concatenate([zero, sin, zero2], axis=1)
    return c, sa, sb


def kernel(x, a_pre_g, a_post_g, a_w_dq_dkv, a_q_norm_g, a_kv_norm_g, a_w_uq, a_w_ukv, a_w_o,
           kv_norm_g, w_kv_shared, b_pre_g, b_post_g, b_w_q, b_w_o,
           mlp_pre_g, mlp_post_g, mlp_w_up, mlp_w_down):
    bsz, seq, d = x.shape
    n_a = a_pre_g.shape[0]
    n_b = b_pre_g.shape[0]
    assert n_a >= 1, "the MLP weight cast rides on the first MLA layer's attention call"
    h = x.reshape(bsz * seq, d)
    rope_c, rope_sa, rope_sb = _rope_tables(seq)
    wd, wq, wkv = a_w_dq_dkv.astype(BF16), a_w_uq.astype(BF16), a_w_ukv.astype(BF16)
    w_up = w_down = a_wo = b_wq = b_wo = w_kv = None
    kv_sh = None
    for layer in range(n_a + n_b):
        if layer < n_a:
            i = layer
            q, k, v = mla_front(h, a_pre_g, wd, a_q_norm_g, a_kv_norm_g, wq, wkv, i,
                                rope_c, rope_sa, rope_sb, seq=seq)
            if layer == 0:
                o, w_up, w_down, a_wo, b_wq, b_wo, w_kv = softmax_attention(
                    q, k, v, batch=bsz, seq=seq, heads=A_HEADS, dk=QK_PAD, dv=V_DIM,
                    cast_f32=[(mlp_w_up, 0), (mlp_w_down, 0), (a_w_o.reshape(1, -1, d), 0),
                              (b_w_q.reshape(1, n_b * d, -1), 0), (b_w_o.reshape(1, -1, d), 0),
                              (w_kv_shared[None], 0)])
                a_wo = a_wo.reshape(n_a, -1, d)
                b_wq = b_wq.reshape(n_b, d, -1)
                b_wo = b_wo.reshape(n_b, -1, d)
            else:
                o, = softmax_attention(q, k, v, batch=bsz, seq=seq, heads=A_HEADS, dk=QK_PAD, dv=V_DIM)
            h = proj_residual(o, a_wo, a_post_g, i, h)
        else:
            j = layer - n_a
            q = norm_matmul(h, b_pre_g, b_wq, j, scale=LOG2E * B_HEAD_DIM ** -0.5)
            o = stick_attention(q, kv_sh, batch=bsz, seq=seq, heads=B_HEADS, dh=B_HEAD_DIM)
            h = proj_residual(o, b_wo, b_post_g, j, h)
        if layer + 1 < n_a + n_b:
            h, w_up, w_down = mlp(h, mlp_pre_g, w_up, w_down, mlp_post_g, layer, 0,
                                  (mlp_w_up, mlp_w_down, layer + 1))
        else:
            h, = mlp(h, mlp_pre_g, w_up, w_down, mlp_post_g, layer, 0)
        if layer == n_a - 1:
            kv_sh = norm_matmul(h, kv_norm_g[None], w_kv, 0)
    return h.reshape(bsz, seq, d)
```

```python
import functools

import jax
import jax.numpy as jnp
from jax import lax
from jax.experimental import pallas as pl
from jax.experimental.pallas import tpu as pltpu

D_MODEL = 2048
A_HEADS = 16
Q_LORA = 512
KV_LORA = 512
NOPE_DIM = 128
ROPE_DIM = 64
V_DIM = 128
ROPE_THETA = 10000.0
B_HEADS = 16
B_HEAD_DIM = 128
NORM_EPS = 1e-6
LOG2E = 1.4426950408889634

LANES = 128
QK_PAD = 256
VMEM_LIMIT = 56 * 1024 * 1024

BF16 = jnp.bfloat16
F32 = jnp.float32


def _params(*sem):
    return pltpu.CompilerParams(dimension_semantics=sem, vmem_limit_bytes=VMEM_LIMIT)


def _rms(x, g):
    return x * lax.rsqrt(jnp.mean(x * x, axis=-1, keepdims=True) + NORM_EPS) * g


def _dot(a, b):
    return jnp.dot(a, b, preferred_element_type=F32)


def _dot_nt(a, b):
    return lax.dot_general(a, b, (((1,), (1,)), ((), ())), preferred_element_type=F32)


def _gain_spec(n, li):
    return pl.BlockSpec((None, 1, n), lambda *_: (li, 0, 0))


def _side_cast(w_all, layer, steps, step_of):
    _, r, c = w_all.shape
    assert r % steps == 0
    rows = r // steps
    return (pl.BlockSpec((None, rows, c), lambda *ids: (layer, step_of(*ids), 0)),
            pl.BlockSpec((None, rows, c), lambda *ids: (0, step_of(*ids), 0)),
            jax.ShapeDtypeStruct((1, r, c), BF16))


def _rope(x, c, sa, sb):
    return x * c + pltpu.roll(x, 96, 1) * sa + pltpu.roll(x, 32, 1) * sb


def _norm_matmul_kernel(x_ref, g_ref, w_ref, o_ref, *, scale):
    y = _dot(_rms(x_ref[...], g_ref[...]).astype(BF16), w_ref[...])
    if scale != 1.0:
        y = y * scale
    o_ref[...] = y.astype(o_ref.dtype)


def norm_matmul(x, g, w, li, *, scale=1.0, tm=512, tn=2048):
    t, d = x.shape
    n = w.shape[2]
    return pl.pallas_call(
        functools.partial(_norm_matmul_kernel, scale=scale),
        grid=(n // tn, t // tm),
        in_specs=[
            pl.BlockSpec((tm, d), lambda j, i: (i, 0)),
            _gain_spec(d, li),
            pl.BlockSpec((None, d, tn), lambda j, i: (li, 0, j)),
        ],
        out_specs=pl.BlockSpec((tm, tn), lambda j, i: (i, j)),
        out_shape=jax.ShapeDtypeStruct((t, n), BF16),
        compiler_params=_params("parallel", "parallel"),
        name="norm_matmul",
    )(x, g.reshape(-1, 1, d), w)


def _mla_front_kernel(h_ref, gpre_ref, wd_ref, gq_ref, gkv_ref, wq_ref, wkv_ref,
                      c_ref, sa_ref, sb_ref, q_ref, k_ref, v_ref, *, scale):
    tm = h_ref.shape[0]
    dqk = NOPE_DIM + ROPE_DIM
    xn = _rms(h_ref[...], gpre_ref[...]).astype(BF16)
    lat = _dot(xn, wd_ref[...])
    cq = _rms(lat[:, :Q_LORA], gq_ref[...]).astype(BF16)
    ckv = _rms(lat[:, Q_LORA:Q_LORA + KV_LORA], gkv_ref[...]).astype(BF16)
    c, sa, sb = c_ref[...], sa_ref[...], sb_ref[...]
    zeros = jnp.zeros((tm, LANES - ROPE_DIM), F32)
    k_rope = _rope(jnp.concatenate([lat[:, Q_LORA + KV_LORA:], zeros], axis=1), c, sa, sb).astype(BF16)
    low = lax.broadcasted_iota(jnp.int32, (tm, LANES), 1) < ROPE_DIM

    def put_q(hh, nope, rope):
        q_ref[:, hh * QK_PAD:hh * QK_PAD + LANES] = (nope * scale).astype(BF16)
        q_ref[:, hh * QK_PAD + LANES:(hh + 1) * QK_PAD] = (_rope(rope, c, sa, sb) * scale).astype(BF16)

    for hh in range(0, A_HEADS, 2):
        q2 = _dot(cq, wq_ref[:, hh * dqk:(hh + 2) * dqk])
        mid, top = q2[:, LANES:2 * LANES], q2[:, 2 * LANES:]
        mid_r, top_r = pltpu.roll(mid, ROPE_DIM, 1), pltpu.roll(top, ROPE_DIM, 1)
        put_q(hh, q2[:, :LANES], jnp.where(low, mid, 0.0))
        put_q(hh + 1, jnp.where(low, mid_r, top_r), jnp.where(low, top_r, 0.0))
    for hh in range(A_HEADS):
        kv = _dot(ckv, wkv_ref[:, hh * 2 * LANES:(hh + 1) * 2 * LANES]).astype(BF16)
        k_ref[:, hh * QK_PAD:hh * QK_PAD + LANES] = kv[:, :LANES]
        k_ref[:, hh * QK_PAD + LANES:(hh + 1) * QK_PAD] = k_rope
        v_ref[:, hh * V_DIM:(hh + 1) * V_DIM] = kv[:, LANES:]


def mla_front(h, g_pre, wd, g_q, g_kv, wq, wkv, li, rope_c, rope_sa, rope_sb, *, seq, tm=512):
    t, d = h.shape
    nqk = A_HEADS * QK_PAD
    nv = A_HEADS * V_DIM
    pos_blocks = seq // tm
    layer = lambda i: (li, 0, 0)
    return pl.pallas_call(
        functools.partial(_mla_front_kernel, scale=LOG2E * (NOPE_DIM + ROPE_DIM) ** -0.5),
        grid=(t // tm,),
        in_specs=[
            pl.BlockSpec((tm, d), lambda i: (i, 0)),
            _gain_spec(d, li),
            pl.BlockSpec((None, d, wd.shape[2]), layer, pipeline_mode=pl.Buffered(1)),
            _gain_spec(Q_LORA, li),
            _gain_spec(KV_LORA, li),
            pl.BlockSpec((None, Q_LORA, wq.shape[2]), layer, pipeline_mode=pl.Buffered(1)),
            pl.BlockSpec((None, KV_LORA, 2 * nv), layer, pipeline_mode=pl.Buffered(1)),
            pl.BlockSpec((tm, LANES), lambda i: (i % pos_blocks, 0)),
            pl.BlockSpec((tm, LANES), lambda i: (i % pos_blocks, 0)),
            pl.BlockSpec((tm, LANES), lambda i: (i % pos_blocks, 0)),
        ],
        out_specs=[
            pl.BlockSpec((tm, nqk), lambda i: (i, 0)),
            pl.BlockSpec((tm, nqk), lambda i: (i, 0)),
            pl.BlockSpec((tm, nv), lambda i: (i, 0)),
        ],
        out_shape=[
            jax.ShapeDtypeStruct((t, nqk), BF16),
            jax.ShapeDtypeStruct((t, nqk), BF16),
            jax.ShapeDtypeStruct((t, nv), BF16),
        ],
        compiler_params=_params("parallel"),
        name="mla_front",
    )(h, g_pre.reshape(-1, 1, d), wd, g_q.reshape(-1, 1, Q_LORA), g_kv.reshape(-1, 1, KV_LORA),
      wq, wkv, rope_c, rope_sa, rope_sb)


def _softmax_attn_kernel(q_ref, k_ref, v_ref, *rest, tq, hp, dk, dv):
    n_cast = len(rest) // 2
    o_ref = rest[n_cast]
    for w_ref, wo_ref in zip(rest[:n_cast], rest[n_cast + 1:]):
        wo_ref[...] = w_ref[...].astype(BF16)
    seq = q_ref.shape[0]
    row = lax.broadcasted_iota(jnp.int32, (tq, tq), 0)
    col = lax.broadcasted_iota(jnp.int32, (tq, tq), 1)
    causal = col <= row

    def scores(hh, i):
        lo, hi = i * tq, (i + 1) * tq
        kc = slice(hh * dk, (hh + 1) * dk)
        q = q_ref[lo:hi, kc]
        s_d = jnp.where(causal, _dot_nt(q, k_ref[lo:hi, kc]), -jnp.inf)
        m = jnp.max(s_d, axis=1, keepdims=True)
        s_o = None
        if i > 0:
            s_o = _dot_nt(q, k_ref[0:lo, kc])
            m = jnp.maximum(m, jnp.max(s_o, axis=1, keepdims=True))
        return s_d, s_o, m

    def weights_times_values(hh, i, s_d, s_o, m):
        lo, hi = i * tq, (i + 1) * tq
        vc = slice(hh * dv, (hh + 1) * dv)
        p_d = jnp.exp2(s_d - m)
        l = jnp.sum(p_d, axis=1, keepdims=True)
        o = _dot(p_d.astype(BF16), v_ref[lo:hi, vc])
        if i > 0:
            p_o = jnp.exp2(s_o - m)
            l = l + jnp.sum(p_o, axis=1, keepdims=True)
            o = o + _dot(p_o.astype(BF16), v_ref[0:lo, vc])
        o_ref[lo:hi, vc] = (o / l).astype(BF16)

    tiles = [(hh, i) for hh in range(hp) for i in range(seq // tq)]
    pending = scores(*tiles[0])
    for n, tile in enumerate(tiles):
        nxt = scores(*tiles[n + 1]) if n + 1 < len(tiles) else None
        weights_times_values(*tile, *pending)
        pending = nxt


def softmax_attention(q, k, v, *, batch, seq, heads, dk, dv, cast_f32=(), tq=256, hp=2):
    t = batch * seq
    ng = heads // hp
    in_specs = [
        pl.BlockSpec((seq, hp * dk), lambda b, g: (b, g)),
        pl.BlockSpec((seq, hp * dk), lambda b, g: (b, g)),
        pl.BlockSpec((seq, hp * dv), lambda b, g: (b, g)),
    ]
    out_specs = [pl.BlockSpec((seq, hp * dv), lambda b, g: (b, g))]
    out_shape = [jax.ShapeDtypeStruct((t, heads * dv), BF16)]
    args = [q, k, v]
    for w_all, layer in cast_f32:
        i_spec, o_spec, o_shape = _side_cast(w_all, layer, batch * ng, lambda b, g: b * ng + g)
        in_specs.append(i_spec)
        out_specs.append(o_spec)
        out_shape.append(o_shape)
        args.append(w_all)
    return pl.pallas_call(
        functools.partial(_softmax_attn_kernel, tq=tq, hp=hp, dk=dk, dv=dv),
        grid=(batch, ng),
        in_specs=in_specs,
        out_specs=out_specs,
        out_shape=out_shape,
        compiler_params=_params("parallel", "parallel"),
        name="softmax_attention",
    )(*args)


def _softplus2(t):
    return jnp.maximum(t, 0.0) + jnp.log(1.0 + jnp.exp2(-jnp.abs(t))) * LOG2E


def _stick_attn_kernel(q_ref, k_ref, v_ref, *rest, tq, hp, dh):
    n_cast = len(rest) // 2
    o_ref = rest[n_cast]
    for w_ref, wo_ref in zip(rest[:n_cast], rest[n_cast + 1:]):
        wo_ref[...] = w_ref[...].astype(BF16)
    seq = q_ref.shape[0]
    reps = tq // LANES
    row = lax.broadcasted_iota(jnp.int32, (tq, tq), 0)
    col = lax.broadcasted_iota(jnp.int32, (tq, tq), 1)
    strict = col < row
    above = jnp.where(row > col, 1.0, 0.0).astype(BF16)

    def scores(hh, i):
        lo, hi = i * tq, (i + 1) * tq
        hc = slice(hh * dh, (hh + 1) * dh)
        z = _dot_nt(q_ref[lo:hi, hc], k_ref[0:hi, hc])
        parts = [None] * (i + 1)
        passed = None
        for b in range(i, -1, -1):
            zb = z[:, b * tq:(b + 1) * tq]
            sp = _softplus2(zb)
            if b == i:
                sp = jnp.where(strict, sp, 0.0)
            sp_bf = sp.astype(BF16)
            excl = _dot(sp_bf, above)
            parts[b] = (zb - sp, excl, passed)
            total = jnp.broadcast_to(excl[:, 0:1] + sp_bf[:, 0:1].astype(F32), (tq, LANES))
            passed = total if passed is None else passed + total
        return parts

    def weights_times_values(hh, i, parts):
        lo, hi = i * tq, (i + 1) * tq
        hc = slice(hh * dh, (hh + 1) * dh)
        a_blocks = []
        for b, (log_beta, excl, passed) in enumerate(parts):
            log_a = log_beta - excl
            if passed is not None:
                log_a = log_a - jnp.tile(passed, (1, reps))
            a = jnp.exp2(log_a)
            if b == i:
                a = jnp.where(strict, a, 0.0)
            a_blocks.append(a.astype(BF16))
        a_all = a_blocks[0] if i == 0 else jnp.concatenate(a_blocks, axis=1)
        o_ref[lo:hi, hc] = _dot(a_all, v_ref[0:hi, hc]).astype(BF16)

    tiles = [(hh, i) for hh in range(hp) for i in range(seq // tq)]
    pending = scores(*tiles[0])
    for n, tile in enumerate(tiles):
        nxt = scores(*tiles[n + 1]) if n + 1 < len(tiles) else None
        weights_times_values(*tile, pending)
        pending = nxt


def stick_attention(q, kv, *, batch, seq, heads, dh, cast_f32=(), tq=256, hp=2):
    t = batch * seq
    ng = heads // hp
    in_specs = [
        pl.BlockSpec((seq, hp * dh), lambda b, g: (b, g)),
        pl.BlockSpec((seq, hp * dh), lambda b, g: (b, g)),
        pl.BlockSpec((seq, hp * dh), lambda b, g: (b, ng + g)),
    ]
    out_specs = [pl.BlockSpec((seq, hp * dh), lambda b, g: (b, g))]
    out_shape = [jax.ShapeDtypeStruct((t, heads * dh), BF16)]
    args = [q, kv, kv]
    for w_all, layer in cast_f32:
        i_spec, o_spec, o_shape = _side_cast(w_all, layer, batch * ng, lambda b, g: b * ng + g)
        in_specs.append(i_spec)
        out_specs.append(o_spec)
        out_shape.append(o_shape)
        args.append(w_all)
    return pl.pallas_call(
        functools.partial(_stick_attn_kernel, tq=tq, hp=hp, dh=dh),
        grid=(batch, ng),
        in_specs=in_specs,
        out_specs=out_specs,
        out_shape=out_shape,
        compiler_params=_params("parallel", "parallel"),
        name="stick_attention",
    )(*args)


def _proj_residual_kernel(o_ref, w_ref, g_ref, h_ref, out_ref):
    a = _dot(o_ref[...], w_ref[...])
    out_ref[...] = h_ref[...] + _rms(a, g_ref[...])


def proj_residual(o, w, g, li, h, *, tm=512):
    t, d = h.shape
    kdim = o.shape[1]
    return pl.pallas_call(
        _proj_residual_kernel,
        grid=(t // tm,),
        in_specs=[
            pl.BlockSpec((tm, kdim), lambda i: (i, 0)),
            pl.BlockSpec((None, kdim, d), lambda i: (li, 0, 0)),
            _gain_spec(d, li),
            pl.BlockSpec((tm, d), lambda i: (i, 0)),
        ],
        out_specs=pl.BlockSpec((tm, d), lambda i: (i, 0)),
        out_shape=jax.ShapeDtypeStruct((t, d), F32),
        compiler_params=_params("parallel"),
        name="proj_residual",
    )(o, w, g.reshape(-1, 1, d), h)


def _mlp_kernel(h_ref, gpre_ref, wup_ref, wdn_ref, gpost_ref, out_ref, xn_ref):
    f = pl.program_id(1)
    last = pl.num_programs(1) - 1

    def chunk():
        u = jnp.maximum(_dot(xn_ref[...], wup_ref[...]), 0.0)
        return _dot((u * u).astype(BF16), wdn_ref[...])

    @pl.when(f == 0)
    def _():
        xn_ref[...] = _rms(h_ref[...], gpre_ref[...]).astype(BF16)
        out_ref[...] = chunk()

    @pl.when((f > 0) & (f < last))
    def _():
        out_ref[...] += chunk()

    @pl.when(f == last)
    def _():
        out_ref[...] = h_ref[...] + _rms(out_ref[...] + chunk(), gpost_ref[...])


def mlp(h, g_pre, w_up, w_down, g_post, li, *, tm=1024, tf=512):
    t, d = h.shape
    ff = w_up.shape[2]
    nt, nf = t // tm, ff // tf
    assert nf >= 2
    return pl.pallas_call(
        _mlp_kernel,
        grid=(nt, nf),
        in_specs=[
            pl.BlockSpec((tm, d), lambda i, f: (i, 0)),
            _gain_spec(d, li),
            pl.BlockSpec((None, d, tf), lambda i, f: (0, 0, f)),
            pl.BlockSpec((None, tf, d), lambda i, f: (0, f, 0)),
            _gain_spec(d, li),
        ],
        out_specs=pl.BlockSpec((tm, d), lambda i, f: (i, 0)),
        out_shape=jax.ShapeDtypeStruct((t, d), F32),
        scratch_shapes=[pltpu.VMEM((tm, d), BF16)],
        compiler_params=_params("parallel", "arbitrary"),
        name="mlp",
    )(h, g_pre.reshape(-1, 1, d), w_up, w_down, g_post.reshape(-1, 1, d))


def _rope_tables(seq):
    pos = jnp.arange(seq, dtype=F32)
    inv_freq = ROPE_THETA ** (-jnp.arange(0, ROPE_DIM, 2, dtype=F32) / ROPE_DIM)
    ang = pos[:, None] * inv_freq[None, :]
    cos, sin = jnp.cos(ang), jnp.sin(ang)
    zero = jnp.zeros_like(cos)
    zero2 = jnp.zeros((seq, LANES - ROPE_DIM), F32)
    c = jnp.concatenate([cos, cos, zero2], axis=1)
    sa = jnp.concatenate([-sin, zero, zero2], axis=1)
    sb = jnp.concatenate([zero, sin, zero2], axis=1)
    return c, sa, sb


def kernel(x, a_pre_g, a_post_g, a_w_dq_dkv, a_q_norm_g, a_kv_norm_g, a_w_uq, a_w_ukv, a_w_o,
           kv_norm_g, w_kv_shared, b_pre_g, b_post_g, b_w_q, b_w_o,
           mlp_pre_g, mlp_post_g, mlp_w_up, mlp_w_down):
    bsz, seq, d = x.shape
    n_a = a_pre_g.shape[0]
    n_b = b_pre_g.shape[0]
    assert n_a >= 1, "the projection weight casts ride on the first MLA layer's attention call"
    h = x.reshape(bsz * seq, d)
    rope_c, rope_sa, rope_sb = _rope_tables(seq)
    wd, wq, wkv = a_w_dq_dkv.astype(BF16), a_w_uq.astype(BF16), a_w_ukv.astype(BF16)
    a_wo = b_wq = b_wo = w_kv = None
    kv_sh = None
    for layer in range(n_a + n_b):
        if layer < n_a:
            i = layer
            q, k, v = mla_front(h, a_pre_g, wd, a_q_norm_g, a_kv_norm_g, wq, wkv, i,
                                rope_c, rope_sa, rope_sb, seq=seq)
            casts = [(mlp_w_up, layer), (mlp_w_down, layer)]
            if layer == 0:
                casts += [(a_w_o.reshape(1, -1, d), 0), (b_w_q.reshape(1, n_b * d, -1), 0),
                          (b_w_o.reshape(1, -1, d), 0), (w_kv_shared[None], 0)]
            o, w_up, w_down, *others = softmax_attention(
                q, k, v, batch=bsz, seq=seq, heads=A_HEADS, dk=QK_PAD, dv=V_DIM, cast_f32=casts)
            if layer == 0:
                a_wo, b_wq, b_wo, w_kv = others
                a_wo = a_wo.reshape(n_a, -1, d)
                b_wq = b_wq.reshape(n_b, d, -1)
                b_wo = b_wo.reshape(n_b, -1, d)
            h = proj_residual(o, a_wo, a_post_g, i, h)
        else:
            j = layer - n_a
            q = norm_matmul(h, b_pre_g, b_wq, j, scale=LOG2E * B_HEAD_DIM ** -0.5)
            o, w_up, w_down = stick_attention(q, kv_sh, batch=bsz, seq=seq, heads=B_HEADS, dh=B_HEAD_DIM,
                                              cast_f32=[(mlp_w_up, layer), (mlp_w_down, layer)])
            h = proj_residual(o, b_wo, b_post_g, j, h)
        h = mlp(h, mlp_pre_g, w_up, w_down, mlp_post_g, layer)
        if layer == n_a - 1:
            kv_sh = norm_matmul(h, kv_norm_g[None], w_kv, 0)
    return h.reshape(bsz, seq, d)
```
